```python
import jax, jax.numpy as jnp
from jax import lax
import numpy as np

D_MODEL = 1024
BATCH = 8
SEQ = 4096
DEPTH = 1

POOL_WINDOWS = (2, 4, 8, 16)
N_POOL_GROUPS = len(POOL_WINDOWS)
POOL_WIDTH = D_MODEL
POOL_GROUP = POOL_WIDTH // N_POOL_GROUPS
HEAD_DIM = 64
N_Q_HEADS = D_MODEL // HEAD_DIM
N_KV_HEADS = 2
GQA_GROUP = N_Q_HEADS // N_KV_HEADS
WINDOW = 128
BLOCK = 128
ROPE_DIM = HEAD_DIM // 4
ROPE_THETA = 500000.0
Q_WIDTH = N_Q_HEADS * HEAD_DIM
KV_WIDTH = N_KV_HEADS * HEAD_DIM
D_FF = 2816
CONV_WIDTH = 3
EPS = 1e-6
IN_WIDTH = POOL_WIDTH + Q_WIDTH + 2 * KV_WIDTH + 2 * D_MODEL

kernel_name = "hybrid_pool_swa_sink_convglu_block"


def rmsnorm(x, g):
    xf = x.astype(jnp.float32)
    r = lax.rsqrt(jnp.mean(xf * xf, axis=-1, keepdims=True) + EPS)
    return (xf * r * g.astype(jnp.float32)).astype(x.dtype)


def partial_rope(x, positions):
    half = ROPE_DIM // 2
    inv_freq = ROPE_THETA ** (-jnp.arange(0, ROPE_DIM, 2, dtype=jnp.float32) / ROPE_DIM)
    ang = positions.astype(jnp.float32)[..., None] * inv_freq
    cos = jnp.cos(ang)[:, :, None, :]
    sin = jnp.sin(ang)[:, :, None, :]
    xf = x.astype(jnp.float32)
    x1, x2, xp = xf[..., :half], xf[..., half:ROPE_DIM], xf[..., ROPE_DIM:]
    out = jnp.concatenate([x1 * cos - x2 * sin, x2 * cos + x1 * sin, xp], axis=-1)
    return out.astype(x.dtype)


def pool_mixer(u, w_pool, pool_scale):
    B, S, _ = u.shape
    ug = u.reshape(B, S, N_POOL_GROUPS, POOL_GROUP).astype(jnp.float32)
    cs = jnp.cumsum(ug, axis=1)
    t = jnp.arange(S, dtype=jnp.float32)
    pooled = []
    for g, w in enumerate(POOL_WINDOWS):
        csg = cs[:, :, g]
        shifted = jnp.pad(csg, ((0, 0), (w, 0), (0, 0)))[:, :S]
        count = jnp.minimum(t + 1.0, float(w))[None, :, None]
        pooled.append((csg - shifted) / count)
    pooled = jnp.stack(pooled, axis=2) - ug
    mixed = jnp.einsum('bsgc,gcd->bsgd', pooled.astype(u.dtype), w_pool)
    return mixed.reshape(B, S, POOL_WIDTH) * pool_scale


def swa_sink_attention(q, k, v, sinks):
    B, S = q.shape[0], q.shape[1]
    nb = S // BLOCK
    qb = q.reshape(B, nb, BLOCK, N_KV_HEADS, GQA_GROUP, HEAD_DIM)

    def band(t):
        tb = t.reshape(B, nb, BLOCK, N_KV_HEADS, HEAD_DIM)
        prev = jnp.pad(tb, ((0, 0), (1, 0), (0, 0), (0, 0), (0, 0)))[:, :-1]
        return jnp.concatenate([prev, tb], axis=2)

    kb, vb = band(k), band(v)
    s = jnp.einsum('bnqhgd,bnkhd->bhgnqk', qb, kb,
                   preferred_element_type=jnp.float32)
    q_pos = jnp.arange(BLOCK)[:, None] + BLOCK
    k_pos = jnp.arange(2 * BLOCK)[None, :]
    rel_ok = (k_pos <= q_pos) & (q_pos - k_pos < WINDOW)
    blk_ok = (jnp.arange(nb)[:, None, None] > 0) | (k_pos[None] >= BLOCK)
    mask = rel_ok[None] & blk_ok
    s = jnp.where(mask, s, -jnp.inf)
    sink = sinks.astype(jnp.float32).reshape(1, N_KV_HEADS, GQA_GROUP, 1, 1, 1)
    m = jnp.maximum(jnp.max(s, axis=-1, keepdims=True), sink)
    p = jnp.exp(s - m)
    denom = jnp.sum(p, axis=-1, keepdims=True) + jnp.exp(sink - m)
    probs = (p / denom).astype(v.dtype)
    out = jnp.einsum('bhgnqk,bnkhd->bnqhgd', probs, vb)
    return out.reshape(B, S, Q_WIDTH)


def causal_depthwise_conv(u, w, b):
    S = u.shape[1]
    up = jnp.pad(u, ((0, 0), (CONV_WIDTH - 1, 0), (0, 0)))
    y = b
    for j in range(CONV_WIDTH):
        y = y + w[j] * up[:, j:j + S]
    return y


def setup_inputs(seed: int = 0) -> dict:
    key = jax.random.key(seed)
    ks = jax.random.split(key, 18)
    f32 = jnp.float32
    nrm = lambda k, shape, s: jax.random.normal(k, shape, f32) * s
    x = jax.random.normal(ks[0], (BATCH, SEQ, D_MODEL), f32)
    offsets = jax.random.randint(ks[1], (BATCH, 1), 0, 1024, dtype=jnp.int32)
    positions = offsets + jnp.arange(SEQ, dtype=jnp.int32)[None, :]
    return {
        "x": x,
        "positions": positions,
        "attn_norm": 1.0 + nrm(ks[2], (DEPTH, D_MODEL), 0.05),
        "w_in": nrm(ks[3], (DEPTH, D_MODEL, IN_WIDTH), D_MODEL ** -0.5),
        "b_gate": nrm(ks[4], (DEPTH, 2 * D_MODEL), 0.1),
        "w_pool": nrm(ks[5], (DEPTH, N_POOL_GROUPS, POOL_GROUP, POOL_GROUP), POOL_GROUP ** -0.5),
        "pool_scale": 1.0 + nrm(ks[6], (DEPTH, POOL_WIDTH), 0.1),
        "q_norm": 1.0 + nrm(ks[7], (DEPTH, HEAD_DIM), 0.05),
        "k_norm": 1.0 + nrm(ks[8], (DEPTH, HEAD_DIM), 0.05),
        "sinks": nrm(ks[9], (DEPTH, N_Q_HEADS), 0.5),
        "w_out": nrm(ks[10], (DEPTH, D_MODEL, D_MODEL), D_MODEL ** -0.5),
        "ffn_norm": 1.0 + nrm(ks[11], (DEPTH, D_MODEL), 0.05),
        "w_up": nrm(ks[12], (DEPTH, D_MODEL, 2 * D_FF), D_MODEL ** -0.5),
        "conv_w": nrm(ks[13], (DEPTH, CONV_WIDTH, 2 * D_FF), CONV_WIDTH ** -0.5),
        "conv_b": nrm(ks[14], (DEPTH, 2 * D_FF), 0.02),
        "w_down": nrm(ks[15], (DEPTH, D_FF, D_MODEL), D_FF ** -0.5),
    }


def reference(x, positions, attn_norm, w_in, b_gate, w_pool, pool_scale, q_norm, k_norm,
              sinks, w_out, ffn_norm, w_up, conv_w, conv_b, w_down):
    B, S, _ = x.shape
    scale = HEAD_DIM ** -0.5
    for l in range(DEPTH):
        h = rmsnorm(x, attn_norm[l])
        z = h @ w_in[l]
        o1 = POOL_WIDTH
        o2 = o1 + Q_WIDTH
        o3 = o2 + KV_WIDTH
        o4 = o3 + KV_WIDTH
        u_pool = z[..., :o1]
        q = z[..., o1:o2].reshape(B, S, N_Q_HEADS, HEAD_DIM)
        k = z[..., o2:o3].reshape(B, S, N_KV_HEADS, HEAD_DIM)
        v = z[..., o3:o4].reshape(B, S, N_KV_HEADS, HEAD_DIM)
        gates = jax.nn.sigmoid((z[..., o4:] + b_gate[l]).astype(jnp.float32)).astype(x.dtype)
        g_pool, g_attn = gates[..., :D_MODEL], gates[..., D_MODEL:]

        a = pool_mixer(u_pool, w_pool[l], pool_scale[l])

        q = partial_rope(rmsnorm(q, q_norm[l]), positions) * scale
        k = partial_rope(rmsnorm(k, k_norm[l]), positions)
        b = swa_sink_attention(q, k, v, sinks[l])

        x = x + (g_pool * a + g_attn * b) @ w_out[l]

        h = rmsnorm(x, ffn_norm[l])
        up = causal_depthwise_conv(h @ w_up[l], conv_w[l], conv_b[l])
        gate, val = up[..., :D_FF], up[..., D_FF:]
        x = x + (jax.nn.silu(gate) * val) @ w_down[l]
    return x
```

```python
import functools

import numpy as np
import jax
import jax.numpy as jnp
from jax import lax
from jax.experimental import pallas as pl
from jax.experimental.pallas import tpu as pltpu

D_MODEL = 1024
POOL_WINDOWS = (2, 4, 8, 16)
POOL_GROUP = 256
POOL_HALO = 16
HEAD_DIM = 64
N_Q_HEADS = 16
N_KV_HEADS = 2
GQA_GROUP = 8
BLOCK = 128
ROPE_DIM = 16
ROPE_HALF = ROPE_DIM // 2
ROPE_THETA = 500000.0
Q_WIDTH = N_Q_HEADS * HEAD_DIM
KV_WIDTH = N_KV_HEADS * HEAD_DIM
D_FF = 2816
CONV_HALO = 8
EPS = 1e-6
IN_WIDTH = D_MODEL + Q_WIDTH + 2 * KV_WIDTH + 2 * D_MODEL
O_Q = D_MODEL
O_K = O_Q + Q_WIDTH
O_V = O_K + KV_WIDTH
O_G = O_V + KV_WIDTH

LANES = 128
MXU_DIM = 256
NEG_BIG = -1e30

SEQ_TILE = 256
FF_CHUNK = 256
VMEM_LIMIT = 56 * 1024 * 1024

BF16 = jnp.bfloat16
F32 = jnp.float32


def _dot(a, b):
    return jnp.dot(a, b, preferred_element_type=F32)


def _dot_nt(a, b):
    return lax.dot_general(a, b, (((1,), (1,)), ((), ())), preferred_element_type=F32)


def _rmsnorm_rows(x, g):
    r = lax.rsqrt(jnp.mean(x * x, axis=-1, keepdims=True) + EPS)
    return x * r * g


def _rope(x, cos_t, sin_fwd, sin_bwd):
    return x * cos_t + pltpu.roll(x, ROPE_HALF, 1) * sin_fwd + pltpu.roll(x, LANES - ROPE_HALF, 1) * sin_bwd


def _mixer_kernel(sinks_ref, x_ref, pos_ref, g1_ref, win_ref, bg_ref, wpool_ref, ps_ref, gq_ref, gk_ref,
                  invf_ref, bd_ref, bias_ref, wout_ref, o_ref,
                  ucarry, kbuf, vbuf, q_scr, b_scr):
    ts = x_ref.shape[1]
    t = pl.program_id(1)

    @pl.when(t == 0)
    def _():
        ucarry[...] = jnp.zeros_like(ucarry)
        kbuf[:, 0:BLOCK, :] = jnp.zeros((4, BLOCK, LANES), BF16)
        vbuf[:, 0:BLOCK, :] = jnp.zeros((4, BLOCK, LANES), BF16)

    x = x_ref[0]
    h = _rmsnorm_rows(x, g1_ref[...]).astype(BF16)
    z = _dot(h, win_ref[...])

    row16 = lax.broadcasted_iota(jnp.int32, (POOL_HALO, POOL_GROUP), 0) + t * ts
    a_parts = []
    for g, w in enumerate(POOL_WINDOWS):
        cols = slice(g * POOL_GROUP, (g + 1) * POOL_GROUP)
        u = z[:, cols]
        ext = jnp.concatenate([ucarry[:, cols], u], axis=0)
        acc = ext
        shift = w // 2
        while shift >= 1:
            acc = acc + pltpu.roll(acc, shift, 0)
            shift //= 2
        wsum = acc[POOL_HALO:]
        cnt = jnp.minimum(row16 + 1, w).astype(F32)
        head = wsum[:POOL_HALO] / cnt
        tail = wsum[POOL_HALO:] * (1.0 / w)
        pooled = jnp.concatenate([head, tail], axis=0) - u
        mixed = _dot(pooled.astype(BF16), wpool_ref[g])
        a_parts.append(mixed * ps_ref[:, cols])
        ucarry[:, cols] = u[ts - POOL_HALO:]

    ang = pos_ref[0] * invf_ref[...]
    lane = lax.broadcasted_iota(jnp.int32, (ts, LANES), 1) % HEAD_DIM
    cosv = jnp.cos(ang)
    sinv = jnp.sin(ang)
    cos_t = jnp.where(lane < ROPE_DIM, cosv, 1.0)
    sin_fwd = jnp.where((lane >= ROPE_HALF) & (lane < ROPE_DIM), sinv, 0.0)
    sin_bwd = jnp.where(lane < ROPE_HALF, -sinv, 0.0)

    for c in range(Q_WIDTH // MXU_DIM):
        qs = z[:, O_Q + c * MXU_DIM:O_Q + (c + 1) * MXU_DIM]
        ss = _dot((qs * qs).astype(BF16), bd_ref[...])
        qn = qs * lax.rsqrt(ss * (1.0 / HEAD_DIM) + EPS) * gq_ref[:, c * MXU_DIM:(c + 1) * MXU_DIM]
        for hf in range(MXU_DIM // LANES):
            qr = _rope(qn[:, hf * LANES:(hf + 1) * LANES], cos_t, sin_fwd, sin_bwd)
            lo = c * MXU_DIM + hf * LANES
            q_scr[:, lo:lo + LANES] = qr.astype(BF16)

    ks = z[:, O_K:O_V]
    ssk = _dot((ks * ks).astype(BF16), bd_ref[0:LANES, 0:LANES])
    kn = ks * lax.rsqrt(ssk * (1.0 / HEAD_DIM) + EPS) * gk_ref[...]
    kr = _rope(kn, cos_t, sin_fwd, sin_bwd)
    vs = z[:, O_V:O_G]
    low = lax.broadcasted_iota(jnp.int32, (ts, LANES), 1) < HEAD_DIM
    for src, buf in ((kr, kbuf), (vs, vbuf)):
        swapped = pltpu.roll(src, HEAD_DIM, 1)
        buf[0, BLOCK:, :] = jnp.where(low, src, 0.0).astype(BF16)
        buf[1, BLOCK:, :] = jnp.where(low, 0.0, swapped).astype(BF16)
        buf[2, BLOCK:, :] = jnp.where(low, swapped, 0.0).astype(BF16)
        buf[3, BLOCK:, :] = jnp.where(low, 0.0, src).astype(BF16)

    pairs = GQA_GROUP // 2
    for j in range(ts // BLOCK):
        if j == 0:
            bias = jnp.where(t == 0, bias_ref[1], bias_ref[0])
        else:
            bias = bias_ref[0]
        rows = slice(j * BLOCK, (j + 1) * BLOCK)
        band = slice(j * BLOCK, j * BLOCK + 2 * BLOCK)
        for kvh in range(N_KV_HEADS):
            base = kvh * GQA_GROUP * HEAD_DIM
            qstack = jnp.concatenate(
                [q_scr[rows, base + p * LANES:base + (p + 1) * LANES] for p in range(pairs)], axis=0)
            out = jnp.zeros((pairs * BLOCK, LANES), F32)
            for par in range(2):
                s = _dot_nt(qstack, kbuf[2 * kvh + par, band, :])
                probs = []
                for p in range(pairs):
                    sp = s[p * BLOCK:(p + 1) * BLOCK] + bias
                    sink = sinks_ref[kvh * GQA_GROUP + 2 * p + par]
                    m = jnp.maximum(jnp.max(sp, axis=-1, keepdims=True), sink)
                    e = jnp.exp(sp - m)
                    denom = jnp.sum(e, axis=-1, keepdims=True) + jnp.exp(sink - m)
                    probs.append((e / denom).astype(BF16))
                out = out + _dot(jnp.concatenate(probs, axis=0), vbuf[2 * kvh + par, band, :])
            for p in range(pairs):
                b_scr[rows, base + p * LANES:base + (p + 1) * LANES] = out[p * BLOCK:(p + 1) * BLOCK]

    kbuf[:, 0:BLOCK, :] = kbuf[:, ts:ts + BLOCK, :]
    vbuf[:, 0:BLOCK, :] = vbuf[:, ts:ts + BLOCK, :]

    a = jnp.concatenate(a_parts, axis=1)
    gates = jax.nn.sigmoid(z[:, O_G:] + bg_ref[...])
    y = gates[:, :D_MODEL] * a + gates[:, D_MODEL:] * b_scr[...]
    o_ref[0] = x + _dot(y.astype(BF16), wout_ref[...])


def _ffn_kernel(x_ref, g2_ref, wup_ref, cw_ref, cb_ref, wdown_ref, o_ref, carry, act_scr):
    ts = x_ref.shape[1]
    t = pl.program_id(1)

    @pl.when(t == 0)
    def _():
        carry[...] = jnp.zeros_like(carry)

    x = x_ref[0]
    h = _rmsnorm_rows(x, g2_ref[...]).astype(BF16)

    def conv(col0):
        cols = slice(col0, col0 + FF_CHUNK)
        up = _dot(h, wup_ref[:, cols])
        ext = jnp.concatenate([carry[:, cols], up], axis=0)
        y = (cb_ref[:, cols] + cw_ref[2:3, cols] * up
             + cw_ref[1:2, cols] * pltpu.roll(ext, 1, 0)[CONV_HALO:]
             + cw_ref[0:1, cols] * pltpu.roll(ext, 2, 0)[CONV_HALO:])
        carry[:, cols] = up[ts - CONV_HALO:]
        return y

    for c in range(D_FF // FF_CHUNK):
        gate = conv(c * FF_CHUNK)
        val = conv(D_FF + c * FF_CHUNK)
        act_scr[:, c * FF_CHUNK:(c + 1) * FF_CHUNK] = (gate * jax.nn.sigmoid(gate) * val).astype(BF16)

    o_ref[0] = x + _dot(act_scr[...], wdown_ref[...])


def _const_spec(shape):
    nd = len(shape)
    return pl.BlockSpec(shape, lambda b, t: (0,) * nd, pipeline_mode=pl.Buffered(1))


def _attention_bias():
    i = np.arange(BLOCK)[:, None]
    c = np.arange(2 * BLOCK)[None, :]
    ok = (c > i) & (c <= i + BLOCK)
    general = np.where(ok, 0.0, NEG_BIG)
    first = np.where(ok & (c >= BLOCK), 0.0, NEG_BIG)
    return jnp.asarray(np.stack([general, first]), F32)


def kernel(x, positions, attn_norm, w_in, b_gate, w_pool, pool_scale, q_norm, k_norm, sinks, w_out, ffn_norm,
           w_up, conv_w, conv_b, w_down):
    B, S, D = x.shape
    assert D == D_MODEL and S % SEQ_TILE == 0 and SEQ_TILE % BLOCK == 0
    assert attn_norm.shape[0] == 1, "single-layer block"
    ts = SEQ_TILE
    grid = (B, S // ts)
    tile = lambda width: pl.BlockSpec((1, ts, width), lambda b, t: (b, t, 0))
    params = pltpu.CompilerParams(dimension_semantics=("arbitrary", "arbitrary"), vmem_limit_bytes=VMEM_LIMIT)

    inv_freq = ROPE_THETA ** (-np.arange(0, ROPE_DIM, 2, dtype=np.float32) / ROPE_DIM)
    lane = np.arange(LANES) % HEAD_DIM
    invf_row = jnp.asarray(np.where(lane < ROPE_DIM, inv_freq[lane % ROPE_HALF], 0.0)[None, :], F32)
    seg = np.arange(MXU_DIM) // HEAD_DIM
    block_diag = jnp.asarray(seg[:, None] == seg[None, :], BF16)
    pos_lanes = jnp.broadcast_to(positions.astype(F32)[..., None], (B, S, LANES))
    gq = jnp.tile(q_norm[0], N_Q_HEADS)[None, :] * (HEAD_DIM ** -0.5)
    gk = jnp.tile(k_norm[0], N_KV_HEADS)[None, :]

    x1 = pl.pallas_call(
        _mixer_kernel,
        grid_spec=pltpu.PrefetchScalarGridSpec(
            num_scalar_prefetch=1,
            grid=grid,
            in_specs=[
                pl.BlockSpec((1, ts, D), lambda b, t, s: (b, t, 0)),
                pl.BlockSpec((1, ts, LANES), lambda b, t, s: (b, t, 0)),
            ] + [pl.BlockSpec(shp, functools.partial(lambda nd, b, t, s: (0,) * nd, len(shp)),
                              pipeline_mode=pl.Buffered(1))
                 for shp in ((1, D), (D, IN_WIDTH), (1, 2 * D), (4, POOL_GROUP, POOL_GROUP), (1, D), (1, Q_WIDTH),
                             (1, KV_WIDTH), (1, LANES), (MXU_DIM, MXU_DIM), (2, BLOCK, 2 * BLOCK), (D, D))],
            out_specs=pl.BlockSpec((1, ts, D), lambda b, t, s: (b, t, 0)),
            scratch_shapes=[
                pltpu.VMEM((POOL_HALO, D), F32),
                pltpu.VMEM((4, ts + BLOCK, LANES), BF16),
                pltpu.VMEM((4, ts + BLOCK, LANES), BF16),
                pltpu.VMEM((ts, Q_WIDTH), BF16),
                pltpu.VMEM((ts, Q_WIDTH), F32),
            ]),
        out_shape=jax.ShapeDtypeStruct((B, S, D), F32),
        compiler_params=params,
        name="mixer",
    )(sinks[0], x, pos_lanes, attn_norm, w_in[0].astype(BF16), b_gate, w_pool[0].astype(BF16), pool_scale, gq, gk,
      invf_row, block_diag, _attention_bias(), w_out[0].astype(BF16))

    out = pl.pallas_call(
        _ffn_kernel,
        grid=grid,
        in_specs=[tile(D), _const_spec((1, D)), _const_spec((D, 2 * D_FF)), _const_spec((3, 2 * D_FF)),
                  _const_spec((1, 2 * D_FF)), _const_spec((D_FF, D))],
        out_specs=tile(D),
        out_shape=jax.ShapeDtypeStruct((B, S, D), F32),
        scratch_shapes=[pltpu.VMEM((CONV_HALO, 2 * D_FF), F32), pltpu.VMEM((ts, D_FF), BF16)],
        compiler_params=params,
        name="ffn",
    )(x1, ffn_norm, w_up[0].astype(BF16), conv_w[0], conv_b, w_down[0].astype(BF16))
    return out
```

```python
import functools

import numpy as np
import jax
import jax.numpy as jnp
from jax import lax
from jax.experimental import pallas as pl
from jax.experimental.pallas import tpu as pltpu

D_MODEL = 1024
POOL_WINDOWS = (2, 4, 8, 16)
POOL_GROUP = 256
POOL_HALO = 16
HEAD_DIM = 64
N_Q_HEADS = 16
N_KV_HEADS = 2
GQA_GROUP = 8
BLOCK = 128
ROPE_DIM = 16
ROPE_HALF = ROPE_DIM // 2
ROPE_THETA = 500000.0
Q_WIDTH = N_Q_HEADS * HEAD_DIM
KV_WIDTH = N_KV_HEADS * HEAD_DIM
D_FF = 2816
CONV_HALO = 8
EPS = 1e-6
IN_WIDTH = D_MODEL + Q_WIDTH + 2 * KV_WIDTH + 2 * D_MODEL
O_Q = D_MODEL
O_K = O_Q + Q_WIDTH
O_V = O_K + KV_WIDTH
O_G = O_V + KV_WIDTH

LANES = 128
MXU_DIM = 256
NEG_BIG = -1e30

MIXER_TILE = 256
FFN_TILE = 512
FF_CHUNK = 256
VMEM_LIMIT = 56 * 1024 * 1024

BF16 = jnp.bfloat16
F32 = jnp.float32


def _dot(a, b):
    return jnp.dot(a, b, preferred_element_type=F32)


def _dot_nt(a, b):
    return lax.dot_general(a, b, (((1,), (1,)), ((), ())), preferred_element_type=F32)


def _rmsnorm_rows(x, g):
    r = lax.rsqrt(jnp.mean(x * x, axis=-1, keepdims=True) + EPS)
    return x * r * g


def _rope(x, cos_t, sin_fwd, sin_bwd):
    return x * cos_t + pltpu.roll(x, ROPE_HALF, 1) * sin_fwd + pltpu.roll(x, LANES - ROPE_HALF, 1) * sin_bwd


def _mixer_kernel(sinks_ref, x_ref, pos_ref, g1_ref, win_ref, bg_ref, wpool_ref, ps_ref, gq_ref, gk_ref,
                  invf_ref, bd_ref, bias_ref, wout_ref, o_ref,
                  ucarry, kbuf, vbuf, q_scr, b_scr):
    ts = x_ref.shape[1]
    t = pl.program_id(1)

    @pl.when(t == 0)
    def _():
        ucarry[...] = jnp.zeros_like(ucarry)
        kbuf[:, 0:BLOCK, :] = jnp.zeros((4, BLOCK, LANES), BF16)
        vbuf[:, 0:BLOCK, :] = jnp.zeros((4, BLOCK, LANES), BF16)

    x = x_ref[0]
    h = _rmsnorm_rows(x, g1_ref[...]).astype(BF16)
    z = _dot(h, win_ref[...])

    row16 = lax.broadcasted_iota(jnp.int32, (POOL_HALO, POOL_GROUP), 0) + t * ts
    a_parts = []
    for g, w in enumerate(POOL_WINDOWS):
        cols = slice(g * POOL_GROUP, (g + 1) * POOL_GROUP)
        u = z[:, cols]
        ext = jnp.concatenate([ucarry[:, cols], u], axis=0)
        acc = ext
        shift = w // 2
        while shift >= 1:
            acc = acc + pltpu.roll(acc, shift, 0)
            shift //= 2
        wsum = acc[POOL_HALO:]
        cnt = jnp.minimum(row16 + 1, w).astype(F32)
        head = wsum[:POOL_HALO] / cnt
        tail = wsum[POOL_HALO:] * (1.0 / w)
        pooled = jnp.concatenate([head, tail], axis=0) - u
        mixed = _dot(pooled.astype(BF16), wpool_ref[g])
        a_parts.append(mixed * ps_ref[:, cols])
        ucarry[:, cols] = u[ts - POOL_HALO:]

    ang = pos_ref[0] * invf_ref[...]
    lane = lax.broadcasted_iota(jnp.int32, (ts, LANES), 1) % HEAD_DIM
    cosv = jnp.cos(ang)
    sinv = jnp.sin(ang)
    cos_t = jnp.where(lane < ROPE_DIM, cosv, 1.0)
    sin_fwd = jnp.where((lane >= ROPE_HALF) & (lane < ROPE_DIM), sinv, 0.0)
    sin_bwd = jnp.where(lane < ROPE_HALF, -sinv, 0.0)

    for c in range(Q_WIDTH // MXU_DIM):
        qs = z[:, O_Q + c * MXU_DIM:O_Q + (c + 1) * MXU_DIM]
        ss = _dot((qs * qs).astype(BF16), bd_ref[...])
        qn = qs * lax.rsqrt(ss * (1.0 / HEAD_DIM) + EPS) * gq_ref[:, c * MXU_DIM:(c + 1) * MXU_DIM]
        for hf in range(MXU_DIM // LANES):
            qr = _rope(qn[:, hf * LANES:(hf + 1) * LANES], cos_t, sin_fwd, sin_bwd)
            lo = c * MXU_DIM + hf * LANES
            q_scr[:, lo:lo + LANES] = qr.astype(BF16)

    ks = z[:, O_K:O_V]
    ssk = _dot((ks * ks).astype(BF16), bd_ref[0:LANES, 0:LANES])
    kn = ks * lax.rsqrt(ssk * (1.0 / HEAD_DIM) + EPS) * gk_ref[...]
    kr = _rope(kn, cos_t, sin_fwd, sin_bwd)
    vs = z[:, O_V:O_G]
    low = lax.broadcasted_iota(jnp.int32, (ts, LANES), 1) < HEAD_DIM
    for src, buf in ((kr, kbuf), (vs, vbuf)):
        swapped = pltpu.roll(src, HEAD_DIM, 1)
        buf[0, BLOCK:, :] = jnp.where(low, src, 0.0).astype(BF16)
        buf[1, BLOCK:, :] = jnp.where(low, 0.0, swapped).astype(BF16)
        buf[2, BLOCK:, :] = jnp.where(low, swapped, 0.0).astype(BF16)
        buf[3, BLOCK:, :] = jnp.where(low, 0.0, src).astype(BF16)

    pairs = GQA_GROUP // 2
    for j in range(ts // BLOCK):
        if j == 0:
            bias = jnp.where(t == 0, bias_ref[1], bias_ref[0])
        else:
            bias = bias_ref[0]
        rows = slice(j * BLOCK, (j + 1) * BLOCK)
        band = slice(j * BLOCK, j * BLOCK + 2 * BLOCK)
        for kvh in range(N_KV_HEADS):
            base = kvh * GQA_GROUP * HEAD_DIM
            qstack = jnp.concatenate(
                [q_scr[rows, base + p * LANES:base + (p + 1) * LANES] for p in range(pairs)], axis=0)
            out = jnp.zeros((pairs * BLOCK, LANES), F32)
            for par in range(2):
                s = _dot_nt(qstack, kbuf[2 * kvh + par, band, :])
                probs = []
                for p in range(pairs):
                    sp = s[p * BLOCK:(p + 1) * BLOCK] + bias
                    sink = sinks_ref[kvh * GQA_GROUP + 2 * p + par]
                    m = jnp.maximum(jnp.max(sp, axis=-1, keepdims=True), sink)
                    e = jnp.exp(sp - m)
                    denom = jnp.sum(e, axis=-1, keepdims=True) + jnp.exp(sink - m)
                    probs.append((e / denom).astype(BF16))
                out = out + _dot(jnp.concatenate(probs, axis=0), vbuf[2 * kvh + par, band, :])
            for p in range(pairs):
                b_scr[rows, base + p * LANES:base + (p + 1) * LANES] = out[p * BLOCK:(p + 1) * BLOCK]

    kbuf[:, 0:BLOCK, :] = kbuf[:, ts:ts + BLOCK, :]
    vbuf[:, 0:BLOCK, :] = vbuf[:, ts:ts + BLOCK, :]

    a = jnp.concatenate(a_parts, axis=1)
    gates = jax.nn.sigmoid(z[:, O_G:] + bg_ref[...])
    y = gates[:, :D_MODEL] * a + gates[:, D_MODEL:] * b_scr[...]
    o_ref[0] = x + _dot(y.astype(BF16), wout_ref[...])


def _ffn_kernel(x_ref, g2_ref, wup_ref, cw_ref, cb_ref, wdown_ref, o_ref, carry, act_scr):
    ts = x_ref.shape[1]
    t = pl.program_id(1)

    @pl.when(t == 0)
    def _():
        carry[...] = jnp.zeros_like(carry)

    x = x_ref[0]
    h = _rmsnorm_rows(x, g2_ref[...]).astype(BF16)

    def conv(col0):
        cols = slice(col0, col0 + FF_CHUNK)
        up = _dot(h, wup_ref[:, cols])
        ext = jnp.concatenate([carry[:, cols], up], axis=0)
        y = (cb_ref[:, cols] + cw_ref[2:3, cols] * up
             + cw_ref[1:2, cols] * pltpu.roll(ext, 1, 0)[CONV_HALO:]
             + cw_ref[0:1, cols] * pltpu.roll(ext, 2, 0)[CONV_HALO:])
        carry[:, cols] = up[ts - CONV_HALO:]
        return y

    for c in range(D_FF // FF_CHUNK):
        gate = conv(c * FF_CHUNK)
        val = conv(D_FF + c * FF_CHUNK)
        act_scr[:, c * FF_CHUNK:(c + 1) * FF_CHUNK] = (gate * jax.nn.sigmoid(gate) * val).astype(BF16)

    o_ref[0] = x + _dot(act_scr[...], wdown_ref[...])


def _const_spec(shape):
    nd = len(shape)
    return pl.BlockSpec(shape, lambda b, t: (0,) * nd, pipeline_mode=pl.Buffered(1))


def _attention_bias():
    i = np.arange(BLOCK)[:, None]
    c = np.arange(2 * BLOCK)[None, :]
    ok = (c > i) & (c <= i + BLOCK)
    general = np.where(ok, 0.0, NEG_BIG)
    first = np.where(ok & (c >= BLOCK), 0.0, NEG_BIG)
    return jnp.asarray(np.stack([general, first]), F32)


def kernel(x, positions, attn_norm, w_in, b_gate, w_pool, pool_scale, q_norm, k_norm, sinks, w_out, ffn_norm,
           w_up, conv_w, conv_b, w_down):
    B, S, D = x.shape
    assert D == D_MODEL and S % MIXER_TILE == 0 and MIXER_TILE % BLOCK == 0 and S % FFN_TILE == 0
    assert attn_norm.shape[0] == 1, "single-layer block"
    ts = MIXER_TILE
    tf = FFN_TILE
    params = pltpu.CompilerParams(dimension_semantics=("arbitrary", "arbitrary"), vmem_limit_bytes=VMEM_LIMIT)

    inv_freq = ROPE_THETA ** (-np.arange(0, ROPE_DIM, 2, dtype=np.float32) / ROPE_DIM)
    lane = np.arange(LANES) % HEAD_DIM
    invf_row = jnp.asarray(np.where(lane < ROPE_DIM, inv_freq[lane % ROPE_HALF], 0.0)[None, :], F32)
    seg = np.arange(MXU_DIM) // HEAD_DIM
    block_diag = jnp.asarray(seg[:, None] == seg[None, :], BF16)
    pos_lanes = jnp.broadcast_to(positions.astype(F32)[..., None], (B, S, LANES))
    gq = jnp.tile(q_norm[0], N_Q_HEADS)[None, :] * (HEAD_DIM ** -0.5)
    gk = jnp.tile(k_norm[0], N_KV_HEADS)[None, :]

    x1 = pl.pallas_call(
        _mixer_kernel,
        grid_spec=pltpu.PrefetchScalarGridSpec(
            num_scalar_prefetch=1,
            grid=(B, S // ts),
            in_specs=[
                pl.BlockSpec((1, ts, D), lambda b, t, s: (b, t, 0)),
                pl.BlockSpec((1, ts, LANES), lambda b, t, s: (b, t, 0)),
            ] + [pl.BlockSpec(shp, functools.partial(lambda nd, b, t, s: (0,) * nd, len(shp)),
                              pipeline_mode=pl.Buffered(1))
                 for shp in ((1, D), (D, IN_WIDTH), (1, 2 * D), (4, POOL_GROUP, POOL_GROUP), (1, D), (1, Q_WIDTH),
                             (1, KV_WIDTH), (1, LANES), (MXU_DIM, MXU_DIM), (2, BLOCK, 2 * BLOCK), (D, D))],
            out_specs=pl.BlockSpec((1, ts, D), lambda b, t, s: (b, t, 0)),
            scratch_shapes=[
                pltpu.VMEM((POOL_HALO, D), F32),
                pltpu.VMEM((4, ts + BLOCK, LANES), BF16),
                pltpu.VMEM((4, ts + BLOCK, LANES), BF16),
                pltpu.VMEM((ts, Q_WIDTH), BF16),
                pltpu.VMEM((ts, Q_WIDTH), F32),
            ]),
        out_shape=jax.ShapeDtypeStruct((B, S, D), F32),
        compiler_params=params,
        name="mixer",
    )(sinks[0], x, pos_lanes, attn_norm, w_in[0].astype(BF16), b_gate, w_pool[0].astype(BF16), pool_scale, gq, gk,
      invf_row, block_diag, _attention_bias(), w_out[0].astype(BF16))

    out = pl.pallas_call(
        _ffn_kernel,
        grid=(B, S // tf),
        in_specs=[pl.BlockSpec((1, tf, D), lambda b, t: (b, t, 0)), _const_spec((1, D)), _const_spec((D, 2 * D_FF)), _const_spec((3, 2 * D_FF)),
                  _const_spec((1, 2 * D_FF)), _const_spec((D_FF, D))],
        out_specs=pl.BlockSpec((1, tf, D), lambda b, t: (b, t, 0)),
        out_shape=jax.ShapeDtypeStruct((B, S, D), F32),
        scratch_shapes=[pltpu.VMEM((CONV_HALO, 2 * D_FF), F32), pltpu.VMEM((tf, D_FF), BF16)],
        compiler_params=params,
        name="ffn",
    )(x1, ffn_norm, w_up[0].astype(BF16), conv_w[0], conv_b, w_down[0].astype(BF16))
    return out
```

```python
import functools

import numpy as np
import jax
import jax.numpy as jnp
from jax import lax
from jax.experimental import pallas as pl
from jax.experimental.pallas import tpu as pltpu

D_MODEL = 1024
POOL_WINDOWS = (2, 4, 8, 16)
POOL_GROUP = 256
POOL_HALO = 16
HEAD_DIM = 64
N_Q_HEADS = 16
N_KV_HEADS = 2
GQA_GROUP = 8
BLOCK = 128
ROPE_DIM = 16
ROPE_HALF = ROPE_DIM // 2
ROPE_THETA = 500000.0
Q_WIDTH = N_Q_HEADS * HEAD_DIM
KV_WIDTH = N_KV_HEADS * HEAD_DIM
D_FF = 2816
CONV_HALO = 8
EPS = 1e-6
IN_WIDTH = D_MODEL + Q_WIDTH + 2 * KV_WIDTH + 2 * D_MODEL
O_Q = D_MODEL
O_K = O_Q + Q_WIDTH
O_V = O_K + KV_WIDTH
O_G = O_V + KV_WIDTH

LANES = 128
MXU_DIM = 256
NEG_BIG = -1e30

MIXER_TILE = 256
FFN_TILE = 512
FF_CHUNK = 256
VMEM_LIMIT = 56 * 1024 * 1024

BF16 = jnp.bfloat16
F32 = jnp.float32


def _dot(a, b):
    return jnp.dot(a, b, preferred_element_type=F32)


def _dot_nt(a, b):
    return lax.dot_general(a, b, (((1,), (1,)), ((), ())), preferred_element_type=F32)


def _rmsnorm_rows(x, g):
    r = lax.rsqrt(jnp.mean(x * x, axis=-1, keepdims=True) + EPS)
    return x * r * g


def _rope(x, cos_t, sin_fwd, sin_bwd):
    return x * cos_t + pltpu.roll(x, ROPE_HALF, 1) * sin_fwd + pltpu.roll(x, LANES - ROPE_HALF, 1) * sin_bwd


def _mixer_kernel(sinks_ref, x_ref, pos_ref, g1_ref, win_ref, bg_ref, wpool_ref, ps_ref, gq_ref, gk_ref,
                  invf_ref, bd_ref, bias_ref, wout_ref, o_ref,
                  ucarry, kbuf, vbuf, q_scr, b_scr):
    ts = x_ref.shape[1]
    t = pl.program_id(1)

    @pl.when(t == 0)
    def _():
        ucarry[...] = jnp.zeros_like(ucarry)
        kbuf[:, 0:BLOCK, :] = jnp.zeros((4, BLOCK, LANES), BF16)
        vbuf[:, 0:BLOCK, :] = jnp.zeros((4, BLOCK, LANES), BF16)

    x = x_ref[0]
    h = _rmsnorm_rows(x, g1_ref[...]).astype(BF16)
    z = _dot(h, win_ref[...])

    row16 = lax.broadcasted_iota(jnp.int32, (POOL_HALO, POOL_GROUP), 0) + t * ts
    a_parts = []
    for g, w in enumerate(POOL_WINDOWS):
        cols = slice(g * POOL_GROUP, (g + 1) * POOL_GROUP)
        u = z[:, cols]
        ext = jnp.concatenate([ucarry[:, cols], u], axis=0)
        acc = ext
        shift = w // 2
        while shift >= 1:
            acc = acc + pltpu.roll(acc, shift, 0)
            shift //= 2
        wsum = acc[POOL_HALO:]
        cnt = jnp.minimum(row16 + 1, w).astype(F32)
        head = wsum[:POOL_HALO] / cnt
        tail = wsum[POOL_HALO:] * (1.0 / w)
        pooled = jnp.concatenate([head, tail], axis=0) - u
        mixed = _dot(pooled.astype(BF16), wpool_ref[g])
        a_parts.append(mixed * ps_ref[:, cols])
        ucarry[:, cols] = u[ts - POOL_HALO:]

    ang = pos_ref[0] * invf_ref[...]
    lane = lax.broadcasted_iota(jnp.int32, (ts, LANES), 1) % HEAD_DIM
    cosv = jnp.cos(ang)
    sinv = jnp.sin(ang)
    cos_t = jnp.where(lane < ROPE_DIM, cosv, 1.0)
    sin_fwd = jnp.where((lane >= ROPE_HALF) & (lane < ROPE_DIM), sinv, 0.0)
    sin_bwd = jnp.where(lane < ROPE_HALF, -sinv, 0.0)

    for c in range(Q_WIDTH // MXU_DIM):
        qs = z[:, O_Q + c * MXU_DIM:O_Q + (c + 1) * MXU_DIM]
        ss = _dot((qs * qs).astype(BF16), bd_ref[...])
        qn = qs * lax.rsqrt(ss * (1.0 / HEAD_DIM) + EPS) * gq_ref[:, c * MXU_DIM:(c + 1) * MXU_DIM]
        for hf in range(MXU_DIM // LANES):
            qr = _rope(qn[:, hf * LANES:(hf + 1) * LANES], cos_t, sin_fwd, sin_bwd)
            lo = c * MXU_DIM + hf * LANES
            q_scr[:, lo:lo + LANES] = qr.astype(BF16)

    ks = z[:, O_K:O_V]
    ssk = _dot((ks * ks).astype(BF16), bd_ref[0:LANES, 0:LANES])
    kn = ks * lax.rsqrt(ssk * (1.0 / HEAD_DIM) + EPS) * gk_ref[...]
    kr = _rope(kn, cos_t, sin_fwd, sin_bwd)
    vs = z[:, O_V:O_G]
    low = lax.broadcasted_iota(jnp.int32, (ts, LANES), 1) < HEAD_DIM
    for src, buf in ((kr, kbuf), (vs, vbuf)):
        swapped = pltpu.roll(src, HEAD_DIM, 1)
        buf[0, BLOCK:, :] = jnp.where(low, src, 0.0).astype(BF16)
        buf[1, BLOCK:, :] = jnp.where(low, 0.0, swapped).astype(BF16)
        buf[2, BLOCK:, :] = jnp.where(low, swapped, 0.0).astype(BF16)
        buf[3, BLOCK:, :] = jnp.where(low, 0.0, src).astype(BF16)

    pairs = GQA_GROUP // 2
    for j in range(ts // BLOCK):
        if j == 0:
            bias = jnp.where(t == 0, bias_ref[1], bias_ref[0])
        else:
            bias = bias_ref[0]
        rows = slice(j * BLOCK, (j + 1) * BLOCK)
        band = slice(j * BLOCK, j * BLOCK + 2 * BLOCK)
        for kvh in range(N_KV_HEADS):
            base = kvh * GQA_GROUP * HEAD_DIM
            qstack = jnp.concatenate(
                [q_scr[rows, base + p * LANES:base + (p + 1) * LANES] for p in range(pairs)], axis=0)
            out = jnp.zeros((pairs * BLOCK, LANES), F32)
            for par in range(2):
                s = _dot_nt(qstack, kbuf[2 * kvh + par, band, :])
                probs = []
                for p in range(pairs):
                    sp = s[p * BLOCK:(p + 1) * BLOCK] + bias
                    sink = sinks_ref[kvh * GQA_GROUP + 2 * p + par]
                    m = jnp.maximum(jnp.max(sp, axis=-1, keepdims=True), sink)
                    e = jnp.exp(sp - m)
                    denom = jnp.sum(e, axis=-1, keepdims=True) + jnp.exp(sink - m)
                    probs.append((e / denom).astype(BF16))
                out = out + _dot(jnp.concatenate(probs, axis=0), vbuf[2 * kvh + par, band, :])
            for p in range(pairs):
                b_scr[rows, base + p * LANES:base + (p + 1) * LANES] = out[p * BLOCK:(p + 1) * BLOCK]

    kbuf[:, 0:BLOCK, :] = kbuf[:, ts:ts + BLOCK, :]
    vbuf[:, 0:BLOCK, :] = vbuf[:, ts:ts + BLOCK, :]

    a = jnp.concatenate(a_parts, axis=1)
    gates = jax.nn.sigmoid(z[:, O_G:] + bg_ref[...])
    y = gates[:, :D_MODEL] * a + gates[:, D_MODEL:] * b_scr[...]
    o_ref[0] = x + _dot(y.astype(BF16), wout_ref[...])


def _ffn_kernel(x_ref, g2_ref, wup_ref, cw_ref, cb_ref, wdown_ref, o_ref, up_scr, act_scr):
    ts = x_ref.shape[1]
    t = pl.program_id(1)

    @pl.when(t == 0)
    def _():
        up_scr[:, 0:CONV_HALO, :] = jnp.zeros((up_scr.shape[0], CONV_HALO, LANES), F32)

    x = x_ref[0]
    h = _rmsnorm_rows(x, g2_ref[...]).astype(BF16)

    def conv(col0):
        up = _dot(h, wup_ref[:, col0:col0 + FF_CHUNK])
        parts = []
        for s in range(FF_CHUNK // LANES):
            slab = (col0 + s * LANES) // LANES
            cols = slice(col0 + s * LANES, col0 + (s + 1) * LANES)
            cur = up[:, s * LANES:(s + 1) * LANES]
            up_scr[slab, CONV_HALO:, :] = cur
            parts.append(cb_ref[:, cols] + cw_ref[2:3, cols] * cur
                         + cw_ref[1:2, cols] * up_scr[slab, CONV_HALO - 1:CONV_HALO - 1 + ts, :]
                         + cw_ref[0:1, cols] * up_scr[slab, CONV_HALO - 2:CONV_HALO - 2 + ts, :])
            up_scr[slab, 0:CONV_HALO, :] = cur[ts - CONV_HALO:]
        return jnp.concatenate(parts, axis=1)

    for c in range(D_FF // FF_CHUNK):
        gate = conv(c * FF_CHUNK)
        val = conv(D_FF + c * FF_CHUNK)
        act_scr[:, c * FF_CHUNK:(c + 1) * FF_CHUNK] = (gate * jax.nn.sigmoid(gate) * val).astype(BF16)

    o_ref[0] = x + _dot(act_scr[...], wdown_ref[...])


def _const_spec(shape):
    nd = len(shape)
    return pl.BlockSpec(shape, lambda b, t: (0,) * nd, pipeline_mode=pl.Buffered(1))


def _attention_bias():
    i = np.arange(BLOCK)[:, None]
    c = np.arange(2 * BLOCK)[None, :]
    ok = (c > i) & (c <= i + BLOCK)
    general = np.where(ok, 0.0, NEG_BIG)
    first = np.where(ok & (c >= BLOCK), 0.0, NEG_BIG)
    return jnp.asarray(np.stack([general, first]), F32)


def kernel(x, positions, attn_norm, w_in, b_gate, w_pool, pool_scale, q_norm, k_norm, sinks, w_out, ffn_norm,
           w_up, conv_w, conv_b, w_down):
    B, S, D = x.shape
    assert D == D_MODEL and S % MIXER_TILE == 0 and MIXER_TILE % BLOCK == 0 and S % FFN_TILE == 0
    assert attn_norm.shape[0] == 1, "single-layer block"
    ts = MIXER_TILE
    tf = FFN_TILE
    params = pltpu.CompilerParams(dimension_semantics=("arbitrary", "arbitrary"), vmem_limit_bytes=VMEM_LIMIT)

    inv_freq = ROPE_THETA ** (-np.arange(0, ROPE_DIM, 2, dtype=np.float32) / ROPE_DIM)
    lane = np.arange(LANES) % HEAD_DIM
    invf_row = jnp.asarray(np.where(lane < ROPE_DIM, inv_freq[lane % ROPE_HALF], 0.0)[None, :], F32)
    seg = np.arange(MXU_DIM) // HEAD_DIM
    block_diag = jnp.asarray(seg[:, None] == seg[None, :], BF16)
    pos_lanes = jnp.broadcast_to(positions.astype(F32)[..., None], (B, S, LANES))
    gq = jnp.tile(q_norm[0], N_Q_HEADS)[None, :] * (HEAD_DIM ** -0.5)
    gk = jnp.tile(k_norm[0], N_KV_HEADS)[None, :]

    x1 = pl.pallas_call(
        _mixer_kernel,
        grid_spec=pltpu.PrefetchScalarGridSpec(
            num_scalar_prefetch=1,
            grid=(B, S // ts),
            in_specs=[
                pl.BlockSpec((1, ts, D), lambda b, t, s: (b, t, 0)),
                pl.BlockSpec((1, ts, LANES), lambda b, t, s: (b, t, 0)),
            ] + [pl.BlockSpec(shp, functools.partial(lambda nd, b, t, s: (0,) * nd, len(shp)),
                              pipeline_mode=pl.Buffered(1))
                 for shp in ((1, D), (D, IN_WIDTH), (1, 2 * D), (4, POOL_GROUP, POOL_GROUP), (1, D), (1, Q_WIDTH),
                             (1, KV_WIDTH), (1, LANES), (MXU_DIM, MXU_DIM), (2, BLOCK, 2 * BLOCK), (D, D))],
            out_specs=pl.BlockSpec((1, ts, D), lambda b, t, s: (b, t, 0)),
            scratch_shapes=[
                pltpu.VMEM((POOL_HALO, D), F32),
                pltpu.VMEM((4, ts + BLOCK, LANES), BF16),
                pltpu.VMEM((4, ts + BLOCK, LANES), BF16),
                pltpu.VMEM((ts, Q_WIDTH), BF16),
                pltpu.VMEM((ts, Q_WIDTH), F32),
            ]),
        out_shape=jax.ShapeDtypeStruct((B, S, D), F32),
        compiler_params=params,
        name="mixer",
    )(sinks[0], x, pos_lanes, attn_norm, w_in[0].astype(BF16), b_gate, w_pool[0].astype(BF16), pool_scale, gq, gk,
      invf_row, block_diag, _attention_bias(), w_out[0].astype(BF16))

    out = pl.pallas_call(
        _ffn_kernel,
        grid=(B, S // tf),
        in_specs=[pl.BlockSpec((1, tf, D), lambda b, t: (b, t, 0)), _const_spec((1, D)), _const_spec((D, 2 * D_FF)), _const_spec((3, 2 * D_FF)),
                  _const_spec((1, 2 * D_FF)), _const_spec((D_FF, D))],
        out_specs=pl.BlockSpec((1, tf, D), lambda b, t: (b, t, 0)),
        out_shape=jax.ShapeDtypeStruct((B, S, D), F32),
        scratch_shapes=[pltpu.VMEM((2 * D_FF // LANES, CONV_HALO + tf, LANES), F32), pltpu.VMEM((tf, D_FF), BF16)],
        compiler_params=params,
        name="ffn",
    )(x1, ffn_norm, w_up[0].astype(BF16), conv_w[0], conv_b, w_down[0].astype(BF16))
    return out
```

```python
import functools
import math

import numpy as np
import jax
import jax.numpy as jnp
from jax import lax
from jax.experimental import pallas as pl
from jax.experimental.pallas import tpu as pltpu

D_MODEL = 1024
POOL_WINDOWS = (2, 4, 8, 16)
POOL_GROUP = 256
POOL_HALO = 32
POOL_HEAD = 16
HEAD_DIM = 64
N_Q_HEADS = 16
N_KV_HEADS = 2
GQA_GROUP = 8
BLOCK = 128
ROPE_DIM = 16
ROPE_HALF = ROPE_DIM // 2
ROPE_THETA = 500000.0
Q_WIDTH = N_Q_HEADS * HEAD_DIM
KV_WIDTH = N_KV_HEADS * HEAD_DIM
D_FF = 2816
CONV_HALO = 8
EPS = 1e-6
IN_WIDTH = D_MODEL + Q_WIDTH + 2 * KV_WIDTH + 2 * D_MODEL
O_Q = D_MODEL
O_K = O_Q + Q_WIDTH
O_V = O_K + KV_WIDTH
O_G = O_V + KV_WIDTH

LANES = 128
SUBLANES = 8
MXU_DIM = 256
NEG_BIG = -1e30
LOG2E = math.log2(math.e)

MIXER_TILE = 256
FFN_TILE = 512
FF_CHUNK = 256
VMEM_LIMIT = 56 * 1024 * 1024

BF16 = jnp.bfloat16
F32 = jnp.float32


def _dot(a, b):
    return jnp.dot(a, b, preferred_element_type=F32)


def _dot_tn(a, b):
    return lax.dot_general(a, b, (((0,), (0,)), ((), ())), preferred_element_type=F32)


def _rmsnorm_rows(x, g):
    r = lax.rsqrt(jnp.mean(x * x, axis=-1, keepdims=True) + EPS)
    return x * r * g


def _rope(x, cos_t, sin_fwd, sin_bwd):
    return x * cos_t + pltpu.roll(x, ROPE_HALF, 1) * sin_fwd + pltpu.roll(x, LANES - ROPE_HALF, 1) * sin_bwd


def _mixer_kernel(sinks_ref, x_ref, pos_ref, g1_ref, win_ref, bg_ref, wpool_ref, ps_ref, gain_ref, invf_ref,
                  expand_ref, passlane_ref, bd_ref, bias_ref, ident_ref, wout_ref, o_ref,
                  pool_scr, kt_prev, kt_cur, v_prev, v_cur, q_scr, b_scr):
    ts = x_ref.shape[1]
    t = pl.program_id(1)
    pairs = GQA_GROUP // 2

    @pl.when(t > 0)
    def _():
        kt_prev[...] = kt_cur[:, :, ts - BLOCK:]
        v_prev[...] = v_cur[:, ts - BLOCK:, :]

    @pl.when(t == 0)
    def _():
        pool_scr[0, :, 0:POOL_HALO, :] = jnp.zeros((D_MODEL // LANES, POOL_HALO, LANES), F32)
        kt_prev[...] = jnp.zeros_like(kt_prev)
        v_prev[:, :, 0:LANES] = jnp.zeros((4, BLOCK, LANES), BF16)
        lane = lax.broadcasted_iota(jnp.int32, (BLOCK, LANES), 1)
        for par in range(2):
            ones = jnp.where((lane >= HEAD_DIM) == (par == 1), 1.0, 0.0).astype(BF16)
            for kvh in range(N_KV_HEADS):
                v_prev[2 * kvh + par, :, LANES:] = ones
                for j in range(ts // BLOCK):
                    v_cur[2 * kvh + par, j * BLOCK:(j + 1) * BLOCK, LANES:] = ones

    x = x_ref[0]
    h = _rmsnorm_rows(x, g1_ref[...]).astype(BF16)
    zq = _dot(h, win_ref[:, O_Q:O_K])
    zkv = _dot(h, win_ref[:, O_K:O_G])

    ang = invf_ref[...] * pos_ref[0].astype(F32)
    cs = jnp.concatenate([jnp.cos(ang), jnp.sin(ang)], axis=0)
    cs_hi = cs.astype(BF16)
    cs_lo = (cs - cs_hi.astype(F32)).astype(BF16)
    tables = _dot_tn(jnp.concatenate([cs_hi, cs_lo], axis=0), expand_ref[...])
    cos_t = tables[:, 0:LANES] + passlane_ref[...]
    sin_fwd = tables[:, LANES:2 * LANES]
    sin_bwd = tables[:, 2 * LANES:3 * LANES]
    q_tabs = (cos_t * gain_ref[0:1, :], sin_fwd * gain_ref[1:2, :], sin_bwd * gain_ref[2:3, :])
    k_tabs = (cos_t * gain_ref[3:4, :], sin_fwd * gain_ref[4:5, :], sin_bwd * gain_ref[5:6, :])

    for c in range(Q_WIDTH // MXU_DIM):
        qs = zq[:, c * MXU_DIM:(c + 1) * MXU_DIM]
        r = lax.rsqrt(_dot((qs * qs).astype(BF16), bd_ref[...]) + EPS)
        for hf in range(MXU_DIM // LANES):
            sl = slice(hf * LANES, (hf + 1) * LANES)
            lo = c * MXU_DIM + hf * LANES
            q_scr[:, lo:lo + LANES] = (_rope(qs[:, sl], *q_tabs) * r[:, sl]).astype(BF16)

    ks = zkv[:, 0:KV_WIDTH]
    rk = lax.rsqrt(_dot((ks * ks).astype(BF16), bd_ref[0:LANES, 0:LANES]) + EPS)
    kt = (_rope(ks, *k_tabs) * rk).T.astype(BF16)
    zero_half = jnp.zeros((HEAD_DIM, ts), BF16)
    for kvh in range(N_KV_HEADS):
        half = kt[kvh * HEAD_DIM:(kvh + 1) * HEAD_DIM]
        kt_cur[2 * kvh] = jnp.concatenate([half, zero_half], axis=0)
        kt_cur[2 * kvh + 1] = jnp.concatenate([zero_half, half], axis=0)
    vs = zkv[:, KV_WIDTH:]
    low = lax.broadcasted_iota(jnp.int32, (ts, LANES), 1) < HEAD_DIM
    v_swapped = pltpu.roll(vs, HEAD_DIM, 1)
    v_cur[0, :, 0:LANES] = jnp.where(low, vs, 0.0).astype(BF16)
    v_cur[1, :, 0:LANES] = jnp.where(low, 0.0, v_swapped).astype(BF16)
    v_cur[2, :, 0:LANES] = jnp.where(low, v_swapped, 0.0).astype(BF16)
    v_cur[3, :, 0:LANES] = jnp.where(low, 0.0, vs).astype(BF16)

    zu = _dot(h, win_ref[:, 0:O_Q])
    row_head = lax.broadcasted_iota(jnp.int32, (POOL_HEAD, LANES), 0) + t * ts
    a_parts = []
    for g, w in enumerate(POOL_WINDOWS):
        n_stage = g + 1
        pooled = []
        for s in range(POOL_GROUP // LANES):
            slab = g * (POOL_GROUP // LANES) + s
            u = zu[:, slab * LANES:(slab + 1) * LANES]
            pool_scr[0, slab, POOL_HALO:, :] = u
            for k in range(1, n_stage):
                sh = 2 ** (k - 1)
                lo = SUBLANES * k
                pool_scr[k, slab, lo:, :] = (pool_scr[k - 1, slab, lo:, :]
                                             + pool_scr[k - 1, slab, lo - sh:POOL_HALO + ts - sh, :])
            sh = w // 2
            wsum = (pool_scr[n_stage - 1, slab, POOL_HALO:, :]
                    + pool_scr[n_stage - 1, slab, POOL_HALO - sh:POOL_HALO + ts - sh, :])
            cnt = jnp.minimum(row_head + 1, w).astype(F32)
            head = wsum[:POOL_HEAD] / cnt
            tail = wsum[POOL_HEAD:] * (1.0 / w)
            pooled.append(jnp.concatenate([head, tail], axis=0) - u)
            pool_scr[0, slab, 0:POOL_HALO, :] = u[ts - POOL_HALO:]
        cols = slice(g * POOL_GROUP, (g + 1) * POOL_GROUP)
        mixed = _dot(jnp.concatenate(pooled, axis=1).astype(BF16), wpool_ref[g])
        a_parts.append(mixed * ps_ref[:, cols])

    gates = jax.nn.sigmoid(_dot(h, win_ref[:, O_G:]) + bg_ref[...])

    lane_low = lax.broadcasted_iota(jnp.int32, (BLOCK, LANES), 1) < HEAD_DIM
    for j in range(ts // BLOCK):
        if j == 0:
            bias = jnp.where(t == 0, bias_ref[1], bias_ref[0]).astype(BF16)
        else:
            bias = bias_ref[0].astype(BF16)
        rows = slice(j * BLOCK, (j + 1) * BLOCK)
        band = slice((j - 1) * BLOCK, (j + 1) * BLOCK)
        for kvh in range(N_KV_HEADS):
            base = kvh * GQA_GROUP * HEAD_DIM
            qstack = jnp.concatenate(
                [q_scr[rows, base + p * LANES:base + (p + 1) * LANES] for p in range(pairs)], axis=0)
            lhs = jnp.concatenate([qstack, ident_ref[...]], axis=1)
            acc = None
            maxes = []
            for par in range(2):
                var = 2 * kvh + par
                if j == 0:
                    k_band = jnp.concatenate([kt_prev[var], kt_cur[var, :, 0:BLOCK]], axis=1)
                    v_band = jnp.concatenate([v_prev[var], v_cur[var, 0:BLOCK, :]], axis=0)
                else:
                    k_band = kt_cur[var, :, band]
                    v_band = v_cur[var, band, :]
                s = _dot(lhs, jnp.concatenate([k_band, bias], axis=0))
                probs = []
                for p in range(pairs):
                    sp = s[p * BLOCK:(p + 1) * BLOCK]
                    sink = sinks_ref[kvh * GQA_GROUP + 2 * p + par] * LOG2E
                    m = jnp.maximum(jnp.max(sp, axis=-1, keepdims=True), sink)
                    maxes.append((m, sink))
                    probs.append(jnp.exp2(sp - m).astype(BF16))
                pv = _dot(jnp.concatenate(probs, axis=0), v_band)
                acc = pv if acc is None else acc + pv
            for p in range(pairs):
                blk = acc[p * BLOCK:(p + 1) * BLOCK]
                (m_e, sink_e), (m_o, sink_o) = maxes[p], maxes[pairs + p]
                denom = blk[:, LANES:] + jnp.where(lane_low, jnp.exp2(sink_e - m_e), jnp.exp2(sink_o - m_o))
                b_scr[rows, base + p * LANES:base + (p + 1) * LANES] = blk[:, :LANES] / denom

    a = jnp.concatenate(a_parts, axis=1)
    y = gates[:, :D_MODEL] * a + gates[:, D_MODEL:] * b_scr[...]
    o_ref[0] = x + _dot(y.astype(BF16), wout_ref[...])


def _ffn_kernel(x_ref, g2_ref, wup_ref, cw_ref, cb_ref, wdown_ref, o_ref, up_scr, act_scr):
    ts = x_ref.shape[1]
    t = pl.program_id(1)

    @pl.when(t == 0)
    def _():
        up_scr[:, 0:CONV_HALO, :] = jnp.zeros((up_scr.shape[0], CONV_HALO, LANES), F32)

    x = x_ref[0]
    h = _rmsnorm_rows(x, g2_ref[...]).astype(BF16)

    def conv(col0):
        up = _dot(h, wup_ref[:, col0:col0 + FF_CHUNK])
        parts = []
        for s in range(FF_CHUNK // LANES):
            slab = (col0 + s * LANES) // LANES
            cols = slice(col0 + s * LANES, col0 + (s + 1) * LANES)
            cur = up[:, s * LANES:(s + 1) * LANES]
            up_scr[slab, CONV_HALO:, :] = cur
            parts.append(cb_ref[:, cols] + cw_ref[2:3, cols] * cur
                         + cw_ref[1:2, cols] * up_scr[slab, CONV_HALO - 1:CONV_HALO - 1 + ts, :]
                         + cw_ref[0:1, cols] * up_scr[slab, CONV_HALO - 2:CONV_HALO - 2 + ts, :])
            up_scr[slab, 0:CONV_HALO, :] = cur[ts - CONV_HALO:]
        return jnp.concatenate(parts, axis=1)

    for c in range(D_FF // FF_CHUNK):
        gate = conv(c * FF_CHUNK)
        val = conv(D_FF + c * FF_CHUNK)
        act_scr[:, c * FF_CHUNK:(c + 1) * FF_CHUNK] = (gate * jax.nn.sigmoid(gate) * val).astype(BF16)

    o_ref[0] = x + _dot(act_scr[...], wdown_ref[...])


def _const_spec(shape):
    nd = len(shape)
    return pl.BlockSpec(shape, lambda b, t: (0,) * nd, pipeline_mode=pl.Buffered(1))


def _attention_bias():
    i = np.arange(BLOCK)[:, None]
    c = np.arange(2 * BLOCK)[None, :]
    ok = (c > i) & (c <= i + BLOCK)
    general = np.where(ok, 0.0, NEG_BIG)
    first = np.where(ok & (c >= BLOCK), 0.0, NEG_BIG)
    return jnp.asarray(np.stack([general, first]), F32)


def _rope_constants():
    inv_freq = ROPE_THETA ** (-np.arange(0, ROPE_DIM, 2, dtype=np.float32) / ROPE_DIM)
    d = np.arange(LANES) % HEAD_DIM
    expand = np.zeros((4 * ROPE_HALF, 3 * LANES), np.float32)
    for f in range(ROPE_HALF):
        for part in (0, 2 * ROPE_HALF):
            expand[part + f, np.nonzero((d == f) | (d == f + ROPE_HALF))[0]] = 1.0
            expand[part + ROPE_HALF + f, LANES + np.nonzero(d == f + ROPE_HALF)[0]] = 1.0
            expand[part + ROPE_HALF + f, 2 * LANES + np.nonzero(d == f)[0]] = -1.0
    passlane = (d >= ROPE_DIM).astype(np.float32)[None, :]
    return (jnp.asarray(inv_freq[:, None], F32), jnp.asarray(expand, BF16), jnp.asarray(passlane, F32))


def _rope_gains(q_norm, k_norm):
    def rows(g):
        return [jnp.tile(g, LANES // HEAD_DIM), jnp.tile(jnp.roll(g, ROPE_HALF), LANES // HEAD_DIM),
                jnp.tile(jnp.roll(g, -ROPE_HALF), LANES // HEAD_DIM)]
    return jnp.stack(rows(q_norm * (HEAD_DIM ** -0.5 * LOG2E)) + rows(k_norm)).astype(F32)


def kernel(x, positions, attn_norm, w_in, b_gate, w_pool, pool_scale, q_norm, k_norm, sinks, w_out, ffn_norm,
           w_up, conv_w, conv_b, w_down):
    B, S, D = x.shape
    assert D == D_MODEL and S % MIXER_TILE == 0 and MIXER_TILE % BLOCK == 0 and S % FFN_TILE == 0
    assert attn_norm.shape[0] == 1, "single-layer block"
    ts = MIXER_TILE
    tf = FFN_TILE
    params = pltpu.CompilerParams(dimension_semantics=("arbitrary", "arbitrary"), vmem_limit_bytes=VMEM_LIMIT)

    invf_col, expand, passlane = _rope_constants()
    seg = np.arange(MXU_DIM) // HEAD_DIM
    block_mean = jnp.asarray((seg[:, None] == seg[None, :]) / HEAD_DIM, BF16)
    ident = jnp.asarray(np.tile(np.eye(BLOCK, dtype=np.float32), (GQA_GROUP // 2, 1)), BF16)
    n_stage = len(POOL_WINDOWS)

    const_shapes = ((1, D), (D, IN_WIDTH), (1, 2 * D), (4, POOL_GROUP, POOL_GROUP), (1, D), (6, LANES),
                    (ROPE_HALF, 1), (4 * ROPE_HALF, 3 * LANES), (1, LANES), (MXU_DIM, MXU_DIM),
                    (2, BLOCK, 2 * BLOCK), (GQA_GROUP // 2 * BLOCK, BLOCK), (D, D))
    x1 = pl.pallas_call(
        _mixer_kernel,
        grid_spec=pltpu.PrefetchScalarGridSpec(
            num_scalar_prefetch=1,
            grid=(B, S // ts),
            in_specs=[
                pl.BlockSpec((1, ts, D), lambda b, t, s: (b, t, 0)),
                pl.BlockSpec((1, 1, ts), lambda b, t, s: (b, 0, t)),
            ] + [pl.BlockSpec(shp, functools.partial(lambda nd, b, t, s: (0,) * nd, len(shp)),
                              pipeline_mode=pl.Buffered(1)) for shp in const_shapes],
            out_specs=pl.BlockSpec((1, ts, D), lambda b, t, s: (b, t, 0)),
            scratch_shapes=[
                pltpu.VMEM((n_stage, D // LANES, POOL_HALO + ts, LANES), F32),
                pltpu.VMEM((4, LANES, BLOCK), BF16),
                pltpu.VMEM((4, LANES, ts), BF16),
                pltpu.VMEM((4, BLOCK, 2 * LANES), BF16),
                pltpu.VMEM((4, ts, 2 * LANES), BF16),
                pltpu.VMEM((ts, Q_WIDTH), BF16),
                pltpu.VMEM((ts, Q_WIDTH), F32),
            ]),
        out_shape=jax.ShapeDtypeStruct((B, S, D), F32),
        compiler_params=params,
        name="mixer",
    )(sinks[0], x, positions.reshape(B, 1, S), attn_norm, w_in[0].astype(BF16), b_gate, w_pool[0].astype(BF16),
      pool_scale, _rope_gains(q_norm[0], k_norm[0]), invf_col, expand, passlane, block_mean, _attention_bias(),
      ident, w_out[0].astype(BF16))

    out = pl.pallas_call(
        _ffn_kernel,
        grid=(B, S // tf),
        in_specs=[pl.BlockSpec((1, tf, D), lambda b, t: (b, t, 0)), _const_spec((1, D)), _const_spec((D, 2 * D_FF)),
                  _const_spec((3, 2 * D_FF)), _const_spec((1, 2 * D_FF)), _const_spec((D_FF, D))],
        out_specs=pl.BlockSpec((1, tf, D), lambda b, t: (b, t, 0)),
        out_shape=jax.ShapeDtypeStruct((B, S, D), F32),
        scratch_shapes=[pltpu.VMEM((2 * D_FF // LANES, CONV_HALO + tf, LANES), F32), pltpu.VMEM((tf, D_FF), BF16)],
        compiler_params=params,
        name="ffn",
    )(x1, ffn_norm, w_up[0].astype(BF16), conv_w[0], conv_b, w_down[0].astype(BF16))
    return out
```

```python
import functools
import math

import numpy as np
import jax
import jax.numpy as jnp
from jax import lax
from jax.experimental import pallas as pl
from jax.experimental.pallas import tpu as pltpu

D_MODEL = 1024
POOL_WINDOWS = (2, 4, 8, 16)
POOL_GROUP = 256
POOL_HALO = 32
POOL_HEAD = 16
HEAD_DIM = 64
N_Q_HEADS = 16
N_KV_HEADS = 2
GQA_GROUP = 8
BLOCK = 128
ROPE_DIM = 16
ROPE_HALF = ROPE_DIM // 2
ROPE_THETA = 500000.0
Q_WIDTH = N_Q_HEADS * HEAD_DIM
KV_WIDTH = N_KV_HEADS * HEAD_DIM
D_FF = 2816
CONV_HALO = 8
EPS = 1e-6
IN_WIDTH = D_MODEL + Q_WIDTH + 2 * KV_WIDTH + 2 * D_MODEL
O_Q = D_MODEL
O_K = O_Q + Q_WIDTH
O_V = O_K + KV_WIDTH
O_G = O_V + KV_WIDTH

LANES = 128
SUBLANES = 8
MXU_DIM = 256
NEG_BIG = -1e30
LOG2E = math.log2(math.e)

MIXER_TILE = 512
FFN_TILE = 512
FF_CHUNK = 256
VMEM_LIMIT = 56 * 1024 * 1024

BF16 = jnp.bfloat16
F32 = jnp.float32


def _dot(a, b):
    return jnp.dot(a, b, preferred_element_type=F32)


def _dot_tn(a, b):
    return lax.dot_general(a, b, (((0,), (0,)), ((), ())), preferred_element_type=F32)


def _rmsnorm_rows(x, g):
    r = lax.rsqrt(jnp.mean(x * x, axis=-1, keepdims=True) + EPS)
    return x * r * g


def _rope(x, cos_t, sin_fwd, sin_bwd):
    return x * cos_t + pltpu.roll(x, ROPE_HALF, 1) * sin_fwd + pltpu.roll(x, LANES - ROPE_HALF, 1) * sin_bwd


def _mixer_kernel(sinks_ref, x_ref, pos_ref, g1_ref, win_ref, bg_ref, wpool_ref, ps_ref, gain_ref, invf_ref,
                  expand_ref, passlane_ref, bd_ref, bias_ref, ident_ref, wout_ref, o_ref,
                  pool_scr, kt_prev, kt_cur, v_prev, v_cur, q_scr, b_scr):
    ts = x_ref.shape[1]
    t = pl.program_id(1)
    pairs = GQA_GROUP // 2

    @pl.when(t > 0)
    def _():
        kt_prev[...] = kt_cur[:, :, ts - BLOCK:]
        v_prev[...] = v_cur[:, ts - BLOCK:, :]

    @pl.when(t == 0)
    def _():
        pool_scr[0, :, 0:POOL_HALO, :] = jnp.zeros((D_MODEL // LANES, POOL_HALO, LANES), F32)
        kt_prev[...] = jnp.zeros_like(kt_prev)
        v_prev[:, :, 0:LANES] = jnp.zeros((4, BLOCK, LANES), BF16)
        lane = lax.broadcasted_iota(jnp.int32, (BLOCK, LANES), 1)
        for par in range(2):
            ones = jnp.where((lane >= HEAD_DIM) == (par == 1), 1.0, 0.0).astype(BF16)
            for kvh in range(N_KV_HEADS):
                v_prev[2 * kvh + par, :, LANES:] = ones
                for j in range(ts // BLOCK):
                    v_cur[2 * kvh + par, j * BLOCK:(j + 1) * BLOCK, LANES:] = ones

    x = x_ref[0]
    h = _rmsnorm_rows(x, g1_ref[...]).astype(BF16)
    zq = _dot(h, win_ref[:, O_Q:O_K])
    zkv = _dot(h, win_ref[:, O_K:O_G])

    ang = invf_ref[...] * pos_ref[0].astype(F32)
    cs = jnp.concatenate([jnp.cos(ang), jnp.sin(ang)], axis=0)
    cs_hi = cs.astype(BF16)
    cs_lo = (cs - cs_hi.astype(F32)).astype(BF16)
    tables = _dot_tn(jnp.concatenate([cs_hi, cs_lo], axis=0), expand_ref[...])
    cos_t = tables[:, 0:LANES] + passlane_ref[...]
    sin_fwd = tables[:, LANES:2 * LANES]
    sin_bwd = tables[:, 2 * LANES:3 * LANES]
    q_tabs = (cos_t * gain_ref[0:1, :], sin_fwd * gain_ref[1:2, :], sin_bwd * gain_ref[2:3, :])
    k_tabs = (cos_t * gain_ref[3:4, :], sin_fwd * gain_ref[4:5, :], sin_bwd * gain_ref[5:6, :])

    for c in range(Q_WIDTH // MXU_DIM):
        qs = zq[:, c * MXU_DIM:(c + 1) * MXU_DIM]
        r = lax.rsqrt(_dot((qs * qs).astype(BF16), bd_ref[...]) + EPS)
        for hf in range(MXU_DIM // LANES):
            sl = slice(hf * LANES, (hf + 1) * LANES)
            lo = c * MXU_DIM + hf * LANES
            q_scr[:, lo:lo + LANES] = (_rope(qs[:, sl], *q_tabs) * r[:, sl]).astype(BF16)

    ks = zkv[:, 0:KV_WIDTH]
    rk = lax.rsqrt(_dot((ks * ks).astype(BF16), bd_ref[0:LANES, 0:LANES]) + EPS)
    kt = (_rope(ks, *k_tabs) * rk).T.astype(BF16)
    zero_half = jnp.zeros((HEAD_DIM, ts), BF16)
    for kvh in range(N_KV_HEADS):
        half = kt[kvh * HEAD_DIM:(kvh + 1) * HEAD_DIM]
        kt_cur[2 * kvh] = jnp.concatenate([half, zero_half], axis=0)
        kt_cur[2 * kvh + 1] = jnp.concatenate([zero_half, half], axis=0)
    vs = zkv[:, KV_WIDTH:]
    low = lax.broadcasted_iota(jnp.int32, (ts, LANES), 1) < HEAD_DIM
    v_swapped = pltpu.roll(vs, HEAD_DIM, 1)
    v_cur[0, :, 0:LANES] = jnp.where(low, vs, 0.0).astype(BF16)
    v_cur[1, :, 0:LANES] = jnp.where(low, 0.0, v_swapped).astype(BF16)
    v_cur[2, :, 0:LANES] = jnp.where(low, v_swapped, 0.0).astype(BF16)
    v_cur[3, :, 0:LANES] = jnp.where(low, 0.0, vs).astype(BF16)

    zu = _dot(h, win_ref[:, 0:O_Q])
    row_head = lax.broadcasted_iota(jnp.int32, (POOL_HEAD, LANES), 0) + t * ts
    a_parts = []
    for g, w in enumerate(POOL_WINDOWS):
        n_stage = g + 1
        pooled = []
        for s in range(POOL_GROUP // LANES):
            slab = g * (POOL_GROUP // LANES) + s
            u = zu[:, slab * LANES:(slab + 1) * LANES]
            pool_scr[0, slab, POOL_HALO:, :] = u
            for k in range(1, n_stage):
                sh = 2 ** (k - 1)
                lo = SUBLANES * k
                pool_scr[k, slab, lo:, :] = (pool_scr[k - 1, slab, lo:, :]
                                             + pool_scr[k - 1, slab, lo - sh:POOL_HALO + ts - sh, :])
            sh = w // 2
            wsum = (pool_scr[n_stage - 1, slab, POOL_HALO:, :]
                    + pool_scr[n_stage - 1, slab, POOL_HALO - sh:POOL_HALO + ts - sh, :])
            cnt = jnp.minimum(row_head + 1, w).astype(F32)
            head = wsum[:POOL_HEAD] / cnt
            tail = wsum[POOL_HEAD:] * (1.0 / w)
            pooled.append(jnp.concatenate([head, tail], axis=0) - u)
            pool_scr[0, slab, 0:POOL_HALO, :] = u[ts - POOL_HALO:]
        cols = slice(g * POOL_GROUP, (g + 1) * POOL_GROUP)
        mixed = _dot(jnp.concatenate(pooled, axis=1).astype(BF16), wpool_ref[g])
        a_parts.append(mixed * ps_ref[:, cols])

    gates = jax.nn.sigmoid(_dot(h, win_ref[:, O_G:]) + bg_ref[...])

    lane_low = lax.broadcasted_iota(jnp.int32, (BLOCK, LANES), 1) < HEAD_DIM
    for j in range(ts // BLOCK):
        if j == 0:
            bias = jnp.where(t == 0, bias_ref[1], bias_ref[0]).astype(BF16)
        else:
            bias = bias_ref[0].astype(BF16)
        rows = slice(j * BLOCK, (j + 1) * BLOCK)
        band = slice((j - 1) * BLOCK, (j + 1) * BLOCK)
        for kvh in range(N_KV_HEADS):
            base = kvh * GQA_GROUP * HEAD_DIM
            qstack = jnp.concatenate(
                [q_scr[rows, base + p * LANES:base + (p + 1) * LANES] for p in range(pairs)], axis=0)
            lhs = jnp.concatenate([qstack, ident_ref[...]], axis=1)
            acc = None
            maxes = []
            for par in range(2):
                var = 2 * kvh + par
                if j == 0:
                    k_band = jnp.concatenate([kt_prev[var], kt_cur[var, :, 0:BLOCK]], axis=1)
                    v_band = jnp.concatenate([v_prev[var], v_cur[var, 0:BLOCK, :]], axis=0)
                else:
                    k_band = kt_cur[var, :, band]
                    v_band = v_cur[var, band, :]
                s = _dot(lhs, jnp.concatenate([k_band, bias], axis=0))
                probs = []
                for p in range(pairs):
                    sp = s[p * BLOCK:(p + 1) * BLOCK]
                    sink = sinks_ref[kvh * GQA_GROUP + 2 * p + par] * LOG2E
                    m = jnp.maximum(jnp.max(sp, axis=-1, keepdims=True), sink)
                    maxes.append((m, sink))
                    probs.append(jnp.exp2(sp - m).astype(BF16))
                pv = _dot(jnp.concatenate(probs, axis=0), v_band)
                acc = pv if acc is None else acc + pv
            for p in range(pairs):
                blk = acc[p * BLOCK:(p + 1) * BLOCK]
                (m_e, sink_e), (m_o, sink_o) = maxes[p], maxes[pairs + p]
                denom = blk[:, LANES:] + jnp.where(lane_low, jnp.exp2(sink_e - m_e), jnp.exp2(sink_o - m_o))
                b_scr[rows, base + p * LANES:base + (p + 1) * LANES] = blk[:, :LANES] / denom

    a = jnp.concatenate(a_parts, axis=1)
    y = gates[:, :D_MODEL] * a + gates[:, D_MODEL:] * b_scr[...]
    o_ref[0] = x + _dot(y.astype(BF16), wout_ref[...])


def _ffn_kernel(x_ref, g2_ref, wup_ref, cw_ref, cb_ref, wdown_ref, o_ref, up_scr, act_scr):
    ts = x_ref.shape[1]
    t = pl.program_id(1)

    @pl.when(t == 0)
    def _():
        up_scr[:, 0:CONV_HALO, :] = jnp.zeros((up_scr.shape[0], CONV_HALO, LANES), F32)

    x = x_ref[0]
    h = _rmsnorm_rows(x, g2_ref[...]).astype(BF16)

    def conv(col0):
        up = _dot(h, wup_ref[:, col0:col0 + FF_CHUNK])
        parts = []
        for s in range(FF_CHUNK // LANES):
            slab = (col0 + s * LANES) // LANES
            cols = slice(col0 + s * LANES, col0 + (s + 1) * LANES)
            cur = up[:, s * LANES:(s + 1) * LANES]
            up_scr[slab, CONV_HALO:, :] = cur
            parts.append(cb_ref[:, cols] + cw_ref[2:3, cols] * cur
                         + cw_ref[1:2, cols] * up_scr[slab, CONV_HALO - 1:CONV_HALO - 1 + ts, :]
                         + cw_ref[0:1, cols] * up_scr[slab, CONV_HALO - 2:CONV_HALO - 2 + ts, :])
            up_scr[slab, 0:CONV_HALO, :] = cur[ts - CONV_HALO:]
        return jnp.concatenate(parts, axis=1)

    for c in range(D_FF // FF_CHUNK):
        gate = conv(c * FF_CHUNK)
        val = conv(D_FF + c * FF_CHUNK)
        act_scr[:, c * FF_CHUNK:(c + 1) * FF_CHUNK] = (gate * jax.nn.sigmoid(gate) * val).astype(BF16)

    o_ref[0] = x + _dot(act_scr[...], wdown_ref[...])


def _const_spec(shape):
    nd = len(shape)
    return pl.BlockSpec(shape, lambda b, t: (0,) * nd, pipeline_mode=pl.Buffered(1))


def _attention_bias():
    i = np.arange(BLOCK)[:, None]
    c = np.arange(2 * BLOCK)[None, :]
    ok = (c > i) & (c <= i + BLOCK)
    general = np.where(ok, 0.0, NEG_BIG)
    first = np.where(ok & (c >= BLOCK), 0.0, NEG_BIG)
    return jnp.asarray(np.stack([general, first]), F32)


def _rope_constants():
    inv_freq = ROPE_THETA ** (-np.arange(0, ROPE_DIM, 2, dtype=np.float32) / ROPE_DIM)
    d = np.arange(LANES) % HEAD_DIM
    expand = np.zeros((4 * ROPE_HALF, 3 * LANES), np.float32)
    for f in range(ROPE_HALF):
        for part in (0, 2 * ROPE_HALF):
            expand[part + f, np.nonzero((d == f) | (d == f + ROPE_HALF))[0]] = 1.0
            expand[part + ROPE_HALF + f, LANES + np.nonzero(d == f + ROPE_HALF)[0]] = 1.0
            expand[part + ROPE_HALF + f, 2 * LANES + np.nonzero(d == f)[0]] = -1.0
    passlane = (d >= ROPE_DIM).astype(np.float32)[None, :]
    return (jnp.asarray(inv_freq[:, None], F32), jnp.asarray(expand, BF16), jnp.asarray(passlane, F32))


def _rope_gains(q_norm, k_norm):
    def rows(g):
        return [jnp.tile(g, LANES // HEAD_DIM), jnp.tile(jnp.roll(g, ROPE_HALF), LANES // HEAD_DIM),
                jnp.tile(jnp.roll(g, -ROPE_HALF), LANES // HEAD_DIM)]
    return jnp.stack(rows(q_norm * (HEAD_DIM ** -0.5 * LOG2E)) + rows(k_norm)).astype(F32)


def kernel(x, positions, attn_norm, w_in, b_gate, w_pool, pool_scale, q_norm, k_norm, sinks, w_out, ffn_norm,
           w_up, conv_w, conv_b, w_down):
    B, S, D = x.shape
    assert D == D_MODEL and S % MIXER_TILE == 0 and MIXER_TILE % BLOCK == 0 and S % FFN_TILE == 0
    assert attn_norm.shape[0] == 1, "single-layer block"
    ts = MIXER_TILE
    tf = FFN_TILE
    params = pltpu.CompilerParams(dimension_semantics=("arbitrary", "arbitrary"), vmem_limit_bytes=VMEM_LIMIT)

    invf_col, expand, passlane = _rope_constants()
    seg = np.arange(MXU_DIM) // HEAD_DIM
    block_mean = jnp.asarray((seg[:, None] == seg[None, :]) / HEAD_DIM, BF16)
    ident = jnp.asarray(np.tile(np.eye(BLOCK, dtype=np.float32), (GQA_GROUP // 2, 1)), BF16)
    n_stage = len(POOL_WINDOWS)

    const_shapes = ((1, D), (D, IN_WIDTH), (1, 2 * D), (4, POOL_GROUP, POOL_GROUP), (1, D), (6, LANES),
                    (ROPE_HALF, 1), (4 * ROPE_HALF, 3 * LANES), (1, LANES), (MXU_DIM, MXU_DIM),
                    (2, BLOCK, 2 * BLOCK), (GQA_GROUP // 2 * BLOCK, BLOCK), (D, D))
    x1 = pl.pallas_call(
        _mixer_kernel,
        grid_spec=pltpu.PrefetchScalarGridSpec(
            num_scalar_prefetch=1,
            grid=(B, S // ts),
            in_specs=[
                pl.BlockSpec((1, ts, D), lambda b, t, s: (b, t, 0)),
                pl.BlockSpec((1, 1, ts), lambda b, t, s: (b, 0, t)),
            ] + [pl.BlockSpec(shp, functools.partial(lambda nd, b, t, s: (0,) * nd, len(shp)),
                              pipeline_mode=pl.Buffered(1)) for shp in const_shapes],
            out_specs=pl.BlockSpec((1, ts, D), lambda b, t, s: (b, t, 0)),
            scratch_shapes=[
                pltpu.VMEM((n_stage, D // LANES, POOL_HALO + ts, LANES), F32),
                pltpu.VMEM((4, LANES, BLOCK), BF16),
                pltpu.VMEM((4, LANES, ts), BF16),
                pltpu.VMEM((4, BLOCK, 2 * LANES), BF16),
                pltpu.VMEM((4, ts, 2 * LANES), BF16),
                pltpu.VMEM((ts, Q_WIDTH), BF16),
                pltpu.VMEM((ts, Q_WIDTH), F32),
            ]),
        out_shape=jax.ShapeDtypeStruct((B, S, D), F32),
        compiler_params=params,
        name="mixer",
    )(sinks[0], x, positions.reshape(B, 1, S), attn_norm, w_in[0].astype(BF16), b_gate, w_pool[0].astype(BF16),
      pool_scale, _rope_gains(q_norm[0], k_norm[0]), invf_col, expand, passlane, block_mean, _attention_bias(),
      ident, w_out[0].astype(BF16))

    out = pl.pallas_call(
        _ffn_kernel,
        grid=(B, S // tf),
        in_specs=[pl.BlockSpec((1, tf, D), lambda b, t: (b, t, 0)), _const_spec((1, D)), _const_spec((D, 2 * D_FF)),
                  _const_spec((3, 2 * D_FF)), _const_spec((1, 2 * D_FF)), _const_spec((D_FF, D))],
        out_specs=pl.BlockSpec((1, tf, D), lambda b, t: (b, t, 0)),
        out_shape=jax.ShapeDtypeStruct((B, S, D), F32),
        scratch_shapes=[pltpu.VMEM((2 * D_FF // LANES, CONV_HALO + tf, LANES), F32), pltpu.VMEM((tf, D_FF), BF16)],
        compiler_params=params,
        name="ffn",
    )(x1, ffn_norm, w_up[0].astype(BF16), conv_w[0], conv_b, w_down[0].astype(BF16))
    return out
```

```python
import functools
import math

import numpy as np
import jax
import jax.numpy as jnp
from jax import lax
from jax.experimental import pallas as pl
from jax.experimental.pallas import tpu as pltpu

D_MODEL = 1024
POOL_WINDOWS = (2, 4, 8, 16)
POOL_GROUP = 256
POOL_HALO = 32
POOL_HEAD = 16
HEAD_DIM = 64
N_Q_HEADS = 16
N_KV_HEADS = 2
GQA_GROUP = 8
BLOCK = 128
ROPE_DIM = 16
ROPE_HALF = ROPE_DIM // 2
ROPE_THETA = 500000.0
Q_WIDTH = N_Q_HEADS * HEAD_DIM
KV_WIDTH = N_KV_HEADS * HEAD_DIM
D_FF = 2816
CONV_HALO = 8
EPS = 1e-6
IN_WIDTH = D_MODEL + Q_WIDTH + 2 * KV_WIDTH + 2 * D_MODEL
O_Q = D_MODEL
O_K = O_Q + Q_WIDTH
O_V = O_K + KV_WIDTH
O_G = O_V + KV_WIDTH

LANES = 128
SUBLANES = 8
MXU_DIM = 256
NEG_BIG = -1e30
LOG2E = math.log2(math.e)

MIXER_TILE = 512
FFN_TILE = 512
OUT_ROWS = 256
FF_CHUNK = 256
VMEM_LIMIT = 56 * 1024 * 1024

BF16 = jnp.bfloat16
F32 = jnp.float32


def _dot(a, b):
    return jnp.dot(a, b, preferred_element_type=F32)


def _dot_tn(a, b):
    return lax.dot_general(a, b, (((0,), (0,)), ((), ())), preferred_element_type=F32)


def _rmsnorm_rows(x, g):
    r = lax.rsqrt(jnp.mean(x * x, axis=-1, keepdims=True) + EPS)
    return x * r * g


def _rope(x, cos_t, sin_fwd, sin_bwd):
    return x * cos_t + pltpu.roll(x, ROPE_HALF, 1) * sin_fwd + pltpu.roll(x, LANES - ROPE_HALF, 1) * sin_bwd


def _mixer_kernel(sinks_ref, x_ref, pos_ref, g1_ref, win_ref, bg_ref, wpool_ref, ps_ref, gain_ref, invf_ref,
                  expand_ref, passlane_ref, bd_ref, bias_ref, ident_ref, wout_ref, o_ref,
                  pool_scr, kt_prev, kt_cur, v_prev, v_cur, q_scr, b_scr):
    ts = x_ref.shape[1]
    t = pl.program_id(1)
    pairs = GQA_GROUP // 2

    @pl.when(t > 0)
    def _():
        kt_prev[...] = kt_cur[:, :, ts - BLOCK:]
        v_prev[...] = v_cur[:, ts - BLOCK:, :]

    @pl.when(t == 0)
    def _():
        pool_scr[0, :, 0:POOL_HALO, :] = jnp.zeros((D_MODEL // LANES, POOL_HALO, LANES), F32)
        kt_prev[...] = jnp.zeros_like(kt_prev)
        v_prev[:, :, 0:LANES] = jnp.zeros((4, BLOCK, LANES), BF16)
        lane = lax.broadcasted_iota(jnp.int32, (BLOCK, LANES), 1)
        for par in range(2):
            ones = jnp.where((lane >= HEAD_DIM) == (par == 1), 1.0, 0.0).astype(BF16)
            for kvh in range(N_KV_HEADS):
                v_prev[2 * kvh + par, :, LANES:] = ones
                for j in range(ts // BLOCK):
                    v_cur[2 * kvh + par, j * BLOCK:(j + 1) * BLOCK, LANES:] = ones

    x = x_ref[0]
    xg = x * g1_ref[...]
    ms = jnp.mean(x * x, axis=-1, keepdims=True)
    inv = lax.rsqrt(ms + EPS)
    h = (xg * inv).astype(BF16)
    hx = xg.astype(BF16)
    zq = _dot(hx, win_ref[:, O_Q:O_K])
    zkv = _dot(hx, win_ref[:, O_K:O_G])
    eps_row = EPS * (ms + EPS)

    ang = invf_ref[...] * pos_ref[0].astype(F32)
    cs = jnp.concatenate([jnp.cos(ang), jnp.sin(ang)], axis=0)
    cs_hi = cs.astype(BF16)
    cs_lo = (cs - cs_hi.astype(F32)).astype(BF16)
    tables = _dot_tn(jnp.concatenate([cs_hi, cs_lo], axis=0), expand_ref[...])
    cos_t = tables[:, 0:LANES] + passlane_ref[...]
    sin_fwd = tables[:, LANES:2 * LANES]
    sin_bwd = tables[:, 2 * LANES:3 * LANES]
    q_tabs = (cos_t * gain_ref[0:1, :], sin_fwd * gain_ref[1:2, :], sin_bwd * gain_ref[2:3, :])
    k_tabs = (cos_t * gain_ref[3:4, :], sin_fwd * gain_ref[4:5, :], sin_bwd * gain_ref[5:6, :])

    for c in range(Q_WIDTH // MXU_DIM):
        qs = zq[:, c * MXU_DIM:(c + 1) * MXU_DIM]
        r = lax.rsqrt(_dot((qs * qs).astype(BF16), bd_ref[...]) + eps_row)
        for hf in range(MXU_DIM // LANES):
            sl = slice(hf * LANES, (hf + 1) * LANES)
            lo = c * MXU_DIM + hf * LANES
            q_scr[:, lo:lo + LANES] = (_rope(qs[:, sl], *q_tabs) * r[:, sl]).astype(BF16)

    ks = zkv[:, 0:KV_WIDTH]
    rk = lax.rsqrt(_dot((ks * ks).astype(BF16), bd_ref[0:LANES, 0:LANES]) + eps_row)
    kt = (_rope(ks, *k_tabs) * rk).T.astype(BF16)
    zero_half = jnp.zeros((HEAD_DIM, ts), BF16)
    for kvh in range(N_KV_HEADS):
        half = kt[kvh * HEAD_DIM:(kvh + 1) * HEAD_DIM]
        kt_cur[2 * kvh] = jnp.concatenate([half, zero_half], axis=0)
        kt_cur[2 * kvh + 1] = jnp.concatenate([zero_half, half], axis=0)
    vs = zkv[:, KV_WIDTH:] * inv
    low = lax.broadcasted_iota(jnp.int32, (ts, LANES), 1) < HEAD_DIM
    v_swapped = pltpu.roll(vs, HEAD_DIM, 1)
    v_cur[0, :, 0:LANES] = jnp.where(low, vs, 0.0).astype(BF16)
    v_cur[1, :, 0:LANES] = jnp.where(low, 0.0, v_swapped).astype(BF16)
    v_cur[2, :, 0:LANES] = jnp.where(low, v_swapped, 0.0).astype(BF16)
    v_cur[3, :, 0:LANES] = jnp.where(low, 0.0, vs).astype(BF16)

    zu = _dot(h, win_ref[:, 0:O_Q])
    row_head = lax.broadcasted_iota(jnp.int32, (POOL_HEAD, LANES), 0) + t * ts
    a_parts = []
    for g, w in enumerate(POOL_WINDOWS):
        n_stage = g + 1
        pooled = []
        for s in range(POOL_GROUP // LANES):
            slab = g * (POOL_GROUP // LANES) + s
            u = zu[:, slab * LANES:(slab + 1) * LANES]
            pool_scr[0, slab, POOL_HALO:, :] = u
            for k in range(1, n_stage):
                sh = 2 ** (k - 1)
                lo = SUBLANES * k
                pool_scr[k, slab, lo:, :] = (pool_scr[k - 1, slab, lo:, :]
                                             + pool_scr[k - 1, slab, lo - sh:POOL_HALO + ts - sh, :])
            sh = w // 2
            wsum = (pool_scr[n_stage - 1, slab, POOL_HALO:, :]
                    + pool_scr[n_stage - 1, slab, POOL_HALO - sh:POOL_HALO + ts - sh, :])
            cnt = jnp.minimum(row_head + 1, w).astype(F32)
            head = wsum[:POOL_HEAD] / cnt
            tail = wsum[POOL_HEAD:] * (1.0 / w)
            pooled.append(jnp.concatenate([head, tail], axis=0) - u)
            pool_scr[0, slab, 0:POOL_HALO, :] = u[ts - POOL_HALO:]
        cols = slice(g * POOL_GROUP, (g + 1) * POOL_GROUP)
        mixed = _dot(jnp.concatenate(pooled, axis=1).astype(BF16), wpool_ref[g])
        a_parts.append(mixed * ps_ref[:, cols])

    gates = jax.nn.sigmoid(_dot(h, win_ref[:, O_G:]) + bg_ref[...])

    lane_low = lax.broadcasted_iota(jnp.int32, (BLOCK, LANES), 1) < HEAD_DIM
    a = jnp.concatenate(a_parts, axis=1)
    for j in range(ts // BLOCK):
        if j == 0:
            bias = jnp.where(t == 0, bias_ref[1], bias_ref[0]).astype(BF16)
        else:
            bias = bias_ref[0].astype(BF16)
        rows = slice(j * BLOCK, (j + 1) * BLOCK)
        band = slice((j - 1) * BLOCK, (j + 1) * BLOCK)
        for kvh in range(N_KV_HEADS):
            base = kvh * GQA_GROUP * HEAD_DIM
            qstack = jnp.concatenate(
                [q_scr[rows, base + p * LANES:base + (p + 1) * LANES] for p in range(pairs)], axis=0)
            lhs = jnp.concatenate([qstack, ident_ref[...]], axis=1)
            acc = None
            maxes = []
            for par in range(2):
                var = 2 * kvh + par
                if j == 0:
                    k_band = jnp.concatenate([kt_prev[var], kt_cur[var, :, 0:BLOCK]], axis=1)
                    v_band = jnp.concatenate([v_prev[var], v_cur[var, 0:BLOCK, :]], axis=0)
                else:
                    k_band = kt_cur[var, :, band]
                    v_band = v_cur[var, band, :]
                s = _dot(lhs, jnp.concatenate([k_band, bias], axis=0))
                probs = []
                for p in range(pairs):
                    sp = s[p * BLOCK:(p + 1) * BLOCK]
                    sink = sinks_ref[kvh * GQA_GROUP + 2 * p + par] * LOG2E
                    m = jnp.maximum(jnp.max(sp, axis=-1, keepdims=True), sink)
                    maxes.append((m, sink))
                    probs.append(jnp.exp2(sp - m).astype(BF16))
                pv = _dot(jnp.concatenate(probs, axis=0), v_band)
                acc = pv if acc is None else acc + pv
            for p in range(pairs):
                blk = acc[p * BLOCK:(p + 1) * BLOCK]
                (m_e, sink_e), (m_o, sink_o) = maxes[p], maxes[pairs + p]
                denom = blk[:, LANES:] + jnp.where(lane_low, jnp.exp2(sink_e - m_e), jnp.exp2(sink_o - m_o))
                b_scr[rows, base + p * LANES:base + (p + 1) * LANES] = blk[:, :LANES] / denom

        done = (j + 1) * BLOCK
        if done % OUT_ROWS == 0:
            rs = slice(done - OUT_ROWS, done)
            y = gates[rs, :D_MODEL] * a[rs] + gates[rs, D_MODEL:] * b_scr[rs, :]
            o_ref[0, rs, :] = x[rs] + _dot(y.astype(BF16), wout_ref[...])


def _ffn_kernel(x_ref, g2_ref, wup_ref, cw_ref, cb_ref, wdown_ref, o_ref, up_scr, act_scr):
    ts = x_ref.shape[1]
    t = pl.program_id(1)

    @pl.when(t == 0)
    def _():
        up_scr[:, 0:CONV_HALO, :] = jnp.zeros((up_scr.shape[0], CONV_HALO, LANES), F32)

    x = x_ref[0]
    h = _rmsnorm_rows(x, g2_ref[...]).astype(BF16)

    def conv(col0):
        up = _dot(h, wup_ref[:, col0:col0 + FF_CHUNK])
        parts = []
        for s in range(FF_CHUNK // LANES):
            slab = (col0 + s * LANES) // LANES
            cols = slice(col0 + s * LANES, col0 + (s + 1) * LANES)
            cur = up[:, s * LANES:(s + 1) * LANES]
            up_scr[slab, CONV_HALO:, :] = cur
            parts.append(cb_ref[:, cols] + cw_ref[2:3, cols] * cur
                         + cw_ref[1:2, cols] * up_scr[slab, CONV_HALO - 1:CONV_HALO - 1 + ts, :]
                         + cw_ref[0:1, cols] * up_scr[slab, CONV_HALO - 2:CONV_HALO - 2 + ts, :])
            up_scr[slab, 0:CONV_HALO, :] = cur[ts - CONV_HALO:]
        return jnp.concatenate(parts, axis=1)

    for c in range(D_FF // FF_CHUNK):
        gate = conv(c * FF_CHUNK)
        val = conv(D_FF + c * FF_CHUNK)
        act_scr[:, c * FF_CHUNK:(c + 1) * FF_CHUNK] = (gate * jax.nn.sigmoid(gate) * val).astype(BF16)

    o_ref[0] = x + _dot(act_scr[...], wdown_ref[...])


def _const_spec(shape):
    nd = len(shape)
    return pl.BlockSpec(shape, lambda b, t: (0,) * nd, pipeline_mode=pl.Buffered(1))


def _attention_bias():
    i = np.arange(BLOCK)[:, None]
    c = np.arange(2 * BLOCK)[None, :]
    ok = (c > i) & (c <= i + BLOCK)
    general = np.where(ok, 0.0, NEG_BIG)
    first = np.where(ok & (c >= BLOCK), 0.0, NEG_BIG)
    return jnp.asarray(np.stack([general, first]), F32)


def _rope_constants():
    inv_freq = ROPE_THETA ** (-np.arange(0, ROPE_DIM, 2, dtype=np.float32) / ROPE_DIM)
    d = np.arange(LANES) % HEAD_DIM
    expand = np.zeros((4 * ROPE_HALF, 3 * LANES), np.float32)
    for f in range(ROPE_HALF):
        for part in (0, 2 * ROPE_HALF):
            expand[part + f, np.nonzero((d == f) | (d == f + ROPE_HALF))[0]] = 1.0
            expand[part + ROPE_HALF + f, LANES + np.nonzero(d == f + ROPE_HALF)[0]] = 1.0
            expand[part + ROPE_HALF + f, 2 * LANES + np.nonzero(d == f)[0]] = -1.0
    passlane = (d >= ROPE_DIM).astype(np.float32)[None, :]
    return (jnp.asarray(inv_freq[:, None], F32), jnp.asarray(expand, BF16), jnp.asarray(passlane, F32))


def _rope_gains(q_norm, k_norm):
    def rows(g):
        return [jnp.tile(g, LANES // HEAD_DIM), jnp.tile(jnp.roll(g, ROPE_HALF), LANES // HEAD_DIM),
                jnp.tile(jnp.roll(g, -ROPE_HALF), LANES // HEAD_DIM)]
    return jnp.stack(rows(q_norm * (HEAD_DIM ** -0.5 * LOG2E)) + rows(k_norm)).astype(F32)


def kernel(x, positions, attn_norm, w_in, b_gate, w_pool, pool_scale, q_norm, k_norm, sinks, w_out, ffn_norm,
           w_up, conv_w, conv_b, w_down):
    B, S, D = x.shape
    assert D == D_MODEL and S % MIXER_TILE == 0 and MIXER_TILE % BLOCK == 0 and S % FFN_TILE == 0
    assert attn_norm.shape[0] == 1, "single-layer block"
    ts = MIXER_TILE
    tf = FFN_TILE
    params = pltpu.CompilerParams(dimension_semantics=("arbitrary", "arbitrary"), vmem_limit_bytes=VMEM_LIMIT)

    invf_col, expand, passlane = _rope_constants()
    seg = np.arange(MXU_DIM) // HEAD_DIM
    block_mean = jnp.asarray((seg[:, None] == seg[None, :]) / HEAD_DIM, BF16)
    ident = jnp.asarray(np.tile(np.eye(BLOCK, dtype=np.float32), (GQA_GROUP // 2, 1)), BF16)
    n_stage = len(POOL_WINDOWS)

    const_shapes = ((1, D), (D, IN_WIDTH), (1, 2 * D), (4, POOL_GROUP, POOL_GROUP), (1, D), (6, LANES),
                    (ROPE_HALF, 1), (4 * ROPE_HALF, 3 * LANES), (1, LANES), (MXU_DIM, MXU_DIM),
                    (2, BLOCK, 2 * BLOCK), (GQA_GROUP // 2 * BLOCK, BLOCK), (D, D))
    x1 = pl.pallas_call(
        _mixer_kernel,
        grid_spec=pltpu.PrefetchScalarGridSpec(
            num_scalar_prefetch=1,
            grid=(B, S // ts),
            in_specs=[
                pl.BlockSpec((1, ts, D), lambda b, t, s: (b, t, 0)),
                pl.BlockSpec((1, 1, ts), lambda b, t, s: (b, 0, t)),
            ] + [pl.BlockSpec(shp, functools.partial(lambda nd, b, t, s: (0,) * nd, len(shp)),
                              pipeline_mode=pl.Buffered(1)) for shp in const_shapes],
            out_specs=pl.BlockSpec((1, ts, D), lambda b, t, s: (b, t, 0)),
            scratch_shapes=[
                pltpu.VMEM((n_stage, D // LANES, POOL_HALO + ts, LANES), F32),
                pltpu.VMEM((4, LANES, BLOCK), BF16),
                pltpu.VMEM((4, LANES, ts), BF16),
                pltpu.VMEM((4, BLOCK, 2 * LANES), BF16),
                pltpu.VMEM((4, ts, 2 * LANES), BF16),
                pltpu.VMEM((ts, Q_WIDTH), BF16),
                pltpu.VMEM((ts, Q_WIDTH), F32),
            ]),
        out_shape=jax.ShapeDtypeStruct((B, S, D), F32),
        compiler_params=params,
        name="mixer",
    )(sinks[0], x, positions.reshape(B, 1, S), attn_norm, w_in[0].astype(BF16), b_gate, w_pool[0].astype(BF16),
      pool_scale, _rope_gains(q_norm[0], k_norm[0]), invf_col, expand, passlane, block_mean, _attention_bias(),
      ident, w_out[0].astype(BF16))

    out = pl.pallas_call(
        _ffn_kernel,
        grid=(B, S // tf),
        in_specs=[pl.BlockSpec((1, tf, D), lambda b, t: (b, t, 0)), _const_spec((1, D)), _const_spec((D, 2 * D_FF)),
                  _const_spec((3, 2 * D_FF)), _const_spec((1, 2 * D_FF)), _const_spec((D_FF, D))],
        out_specs=pl.BlockSpec((1, tf, D), lambda b, t: (b, t, 0)),
        out_shape=jax.ShapeDtypeStruct((B, S, D), F32),
        scratch_shapes=[pltpu.VMEM((2 * D_FF // LANES, CONV_HALO + tf, LANES), F32), pltpu.VMEM((tf, D_FF), BF16)],
        compiler_params=params,
        name="ffn",
    )(x1, ffn_norm, w_up[0].astype(BF16), conv_w[0], conv_b, w_down[0].astype(BF16))
    return out
```

```python
import functools
import math

import numpy as np
import jax
import jax.numpy as jnp
from jax import lax
from jax.experimental import pallas as pl
from jax.experimental.pallas import tpu as pltpu

D_MODEL = 1024
POOL_WINDOWS = (2, 4, 8, 16)
POOL_GROUP = 256
POOL_HALO = 32
POOL_HEAD = 16
HEAD_DIM = 64
N_Q_HEADS = 16
N_KV_HEADS = 2
GQA_GROUP = 8
BLOCK = 128
ROPE_DIM = 16
ROPE_HALF = ROPE_DIM // 2
ROPE_THETA = 500000.0
Q_WIDTH = N_Q_HEADS * HEAD_DIM
KV_WIDTH = N_KV_HEADS * HEAD_DIM
D_FF = 2816
CONV_HALO = 8
EPS = 1e-6
IN_WIDTH = D_MODEL + Q_WIDTH + 2 * KV_WIDTH + 2 * D_MODEL
O_Q = D_MODEL
O_K = O_Q + Q_WIDTH
O_V = O_K + KV_WIDTH
O_G = O_V + KV_WIDTH

LANES = 128
SUBLANES = 8
MXU_DIM = 256
NEG_BIG = -1e30
LOG2E = math.log2(math.e)

MIXER_TILE = 512
FFN_TILE = 512
FF_CHUNK = 256
VMEM_LIMIT = 56 * 1024 * 1024

BF16 = jnp.bfloat16
F32 = jnp.float32


def _dot(a, b):
    return jnp.dot(a, b, preferred_element_type=F32)


def _dot_tn(a, b):
    return lax.dot_general(a, b, (((0,), (0,)), ((), ())), preferred_element_type=F32)


def _rmsnorm_rows(x, g):
    r = lax.rsqrt(jnp.mean(x * x, axis=-1, keepdims=True) + EPS)
    return x * r * g


def _rope(x, cos_t, sin_fwd, sin_bwd):
    return x * cos_t + pltpu.roll(x, ROPE_HALF, 1) * sin_fwd + pltpu.roll(x, LANES - ROPE_HALF, 1) * sin_bwd


def _mixer_kernel(sinks_ref, x_ref, pos_ref, g1_ref, win_ref, bg_ref, wpool_ref, ps_ref, gain_ref, invf_ref,
                  expand_ref, passlane_ref, bd_ref, bias_ref, ident_ref, wout_ref, o_ref,
                  pool_scr, kt_prev, kt_cur, v_prev, v_cur, q_scr, b_scr):
    ts = x_ref.shape[1]
    t = pl.program_id(1)
    pairs = GQA_GROUP // 2

    @pl.when(t > 0)
    def _():
        kt_prev[...] = kt_cur[:, :, ts - BLOCK:]
        v_prev[...] = v_cur[:, ts - BLOCK:, :]

    @pl.when(t == 0)
    def _():
        pool_scr[0, :, 0:POOL_HALO, :] = jnp.zeros((D_MODEL // LANES, POOL_HALO, LANES), F32)
        kt_prev[...] = jnp.zeros_like(kt_prev)
        v_prev[:, :, 0:LANES] = jnp.zeros((4, BLOCK, LANES), BF16)
        lane = lax.broadcasted_iota(jnp.int32, (BLOCK, LANES), 1)
        for par in range(2):
            ones = jnp.where((lane >= HEAD_DIM) == (par == 1), 1.0, 0.0).astype(BF16)
            for kvh in range(N_KV_HEADS):
                v_prev[2 * kvh + par, :, LANES:] = ones
                for j in range(ts // BLOCK):
                    v_cur[2 * kvh + par, j * BLOCK:(j + 1) * BLOCK, LANES:] = ones

    x = x_ref[0]
    h = _rmsnorm_rows(x, g1_ref[...]).astype(BF16)
    zq = _dot(h, win_ref[:, O_Q:O_K])
    zkv = _dot(h, win_ref[:, O_K:O_G])

    ang = invf_ref[...] * pos_ref[0].astype(F32)
    cs = jnp.concatenate([jnp.cos(ang), jnp.sin(ang)], axis=0)
    cs_hi = cs.astype(BF16)
    cs_lo = (cs - cs_hi.astype(F32)).astype(BF16)
    tables = _dot_tn(jnp.concatenate([cs_hi, cs_lo], axis=0), expand_ref[...])
    cos_t = tables[:, 0:LANES] + passlane_ref[...]
    sin_fwd = tables[:, LANES:2 * LANES]
    sin_bwd = tables[:, 2 * LANES:3 * LANES]
    q_tabs = (cos_t * gain_ref[0:1, :], sin_fwd * gain_ref[1:2, :], sin_bwd * gain_ref[2:3, :])
    k_tabs = (cos_t * gain_ref[3:4, :], sin_fwd * gain_ref[4:5, :], sin_bwd * gain_ref[5:6, :])

    for c in range(Q_WIDTH // MXU_DIM):
        qs = zq[:, c * MXU_DIM:(c + 1) * MXU_DIM]
        r = lax.rsqrt(_dot((qs * qs).astype(BF16), bd_ref[...]) + EPS)
        for hf in range(MXU_DIM // LANES):
            sl = slice(hf * LANES, (hf + 1) * LANES)
            lo = c * MXU_DIM + hf * LANES
            q_scr[:, lo:lo + LANES] = (_rope(qs[:, sl], *q_tabs) * r[:, sl]).astype(BF16)

    ks = zkv[:, 0:KV_WIDTH]
    rk = lax.rsqrt(_dot((ks * ks).astype(BF16), bd_ref[0:LANES, 0:LANES]) + EPS)
    kt = (_rope(ks, *k_tabs) * rk).T.astype(BF16)
    zero_half = jnp.zeros((HEAD_DIM, ts), BF16)
    for kvh in range(N_KV_HEADS):
        half = kt[kvh * HEAD_DIM:(kvh + 1) * HEAD_DIM]
        kt_cur[2 * kvh] = jnp.concatenate([half, zero_half], axis=0)
        kt_cur[2 * kvh + 1] = jnp.concatenate([zero_half, half], axis=0)
    vs = zkv[:, KV_WIDTH:]
    low = lax.broadcasted_iota(jnp.int32, (ts, LANES), 1) < HEAD_DIM
    v_swapped = pltpu.roll(vs, HEAD_DIM, 1)
    v_cur[0, :, 0:LANES] = jnp.where(low, vs, 0.0).astype(BF16)
    v_cur[1, :, 0:LANES] = jnp.where(low, 0.0, v_swapped).astype(BF16)
    v_cur[2, :, 0:LANES] = jnp.where(low, v_swapped, 0.0).astype(BF16)
    v_cur[3, :, 0:LANES] = jnp.where(low, 0.0, vs).astype(BF16)

    zu = _dot(h, win_ref[:, 0:O_Q])
    row_head = lax.broadcasted_iota(jnp.int32, (POOL_HEAD, LANES), 0) + t * ts
    a_parts = []
    for g, w in enumerate(POOL_WINDOWS):
        n_stage = g + 1
        pooled = []
        for s in range(POOL_GROUP // LANES):
            slab = g * (POOL_GROUP // LANES) + s
            u = zu[:, slab * LANES:(slab + 1) * LANES]
            pool_scr[0, slab, POOL_HALO:, :] = u
            for k in range(1, n_stage):
                sh = 2 ** (k - 1)
                lo = SUBLANES * k
                pool_scr[k, slab, lo:, :] = (pool_scr[k - 1, slab, lo:, :]
                                             + pool_scr[k - 1, slab, lo - sh:POOL_HALO + ts - sh, :])
            sh = w // 2
            wsum = (pool_scr[n_stage - 1, slab, POOL_HALO:, :]
                    + pool_scr[n_stage - 1, slab, POOL_HALO - sh:POOL_HALO + ts - sh, :])
            cnt = jnp.minimum(row_head + 1, w).astype(F32)
            head = wsum[:POOL_HEAD] / cnt
            tail = wsum[POOL_HEAD:] * (1.0 / w)
            pooled.append(jnp.concatenate([head, tail], axis=0) - u)
            pool_scr[0, slab, 0:POOL_HALO, :] = u[ts - POOL_HALO:]
        cols = slice(g * POOL_GROUP, (g + 1) * POOL_GROUP)
        mixed = _dot(jnp.concatenate(pooled, axis=1).astype(BF16), wpool_ref[g])
        a_parts.append(mixed * ps_ref[:, cols])

    gates = jax.nn.sigmoid(_dot(h, win_ref[:, O_G:]) + bg_ref[...])

    lane_low = lax.broadcasted_iota(jnp.int32, (BLOCK, LANES), 1) < HEAD_DIM
    a = jnp.concatenate(a_parts, axis=1)
    y_pool = gates[:, :D_MODEL] * a
    bias_first = jnp.where(t == 0, bias_ref[1], bias_ref[0]).astype(BF16)
    bias_rest = bias_ref[0].astype(BF16)
    group_w = GQA_GROUP * HEAD_DIM
    out = x
    for kvh in range(N_KV_HEADS):
        base = kvh * group_w
        for j in range(ts // BLOCK):
            bias = bias_first if j == 0 else bias_rest
            rows = slice(j * BLOCK, (j + 1) * BLOCK)
            band = slice((j - 1) * BLOCK, (j + 1) * BLOCK)
            qstack = jnp.concatenate(
                [q_scr[rows, base + p * LANES:base + (p + 1) * LANES] for p in range(pairs)], axis=0)
            lhs = jnp.concatenate([qstack, ident_ref[...]], axis=1)
            acc = None
            maxes = []
            for par in range(2):
                var = 2 * kvh + par
                if j == 0:
                    k_band = jnp.concatenate([kt_prev[var], kt_cur[var, :, 0:BLOCK]], axis=1)
                    v_band = jnp.concatenate([v_prev[var], v_cur[var, 0:BLOCK, :]], axis=0)
                else:
                    k_band = kt_cur[var, :, band]
                    v_band = v_cur[var, band, :]
                s = _dot(lhs, jnp.concatenate([k_band, bias], axis=0))
                probs = []
                for p in range(pairs):
                    sp = s[p * BLOCK:(p + 1) * BLOCK]
                    m = jnp.max(sp, axis=-1, keepdims=True)
                    maxes.append(m)
                    probs.append(jnp.exp2(sp - m).astype(BF16))
                pv = _dot(jnp.concatenate(probs, axis=0), v_band)
                acc = pv if acc is None else acc + pv
            for p in range(pairs):
                blk = acc[p * BLOCK:(p + 1) * BLOCK]
                head = kvh * GQA_GROUP + 2 * p
                sink = jnp.where(lane_low[0:1], sinks_ref[head] * LOG2E, sinks_ref[head + 1] * LOG2E)
                denom = blk[:, LANES:] + jnp.exp2(sink - jnp.where(lane_low, maxes[p], maxes[pairs + p]))
                b_scr[rows, base + p * LANES:base + (p + 1) * LANES] = blk[:, :LANES] / denom

        cols = slice(base, base + group_w)
        y = y_pool[:, cols] + gates[:, D_MODEL + base:D_MODEL + base + group_w] * b_scr[:, cols]
        out = out + _dot(y.astype(BF16), wout_ref[cols, :])
    o_ref[0] = out


def _ffn_kernel(x_ref, g2_ref, wup_ref, cw_ref, cb_ref, wdown_ref, o_ref, up_scr, act_scr):
    ts = x_ref.shape[1]
    t = pl.program_id(1)

    @pl.when(t == 0)
    def _():
        up_scr[:, 0:CONV_HALO, :] = jnp.zeros((up_scr.shape[0], CONV_HALO, LANES), F32)

    x = x_ref[0]
    h = _rmsnorm_rows(x, g2_ref[...]).astype(BF16)

    def conv(col0):
        up = _dot(h, wup_ref[:, col0:col0 + FF_CHUNK])
        parts = []
        for s in range(FF_CHUNK // LANES):
            slab = (col0 + s * LANES) // LANES
            cols = slice(col0 + s * LANES, col0 + (s + 1) * LANES)
            cur = up[:, s * LANES:(s + 1) * LANES]
            up_scr[slab, CONV_HALO:, :] = cur
            parts.append(cb_ref[:, cols] + cw_ref[2:3, cols] * cur
                         + cw_ref[1:2, cols] * up_scr[slab, CONV_HALO - 1:CONV_HALO - 1 + ts, :]
                         + cw_ref[0:1, cols] * up_scr[slab, CONV_HALO - 2:CONV_HALO - 2 + ts, :])
            up_scr[slab, 0:CONV_HALO, :] = cur[ts - CONV_HALO:]
        return jnp.concatenate(parts, axis=1)

    for c in range(D_FF // FF_CHUNK):
        gate = conv(c * FF_CHUNK)
        val = conv(D_FF + c * FF_CHUNK)
        act_scr[:, c * FF_CHUNK:(c + 1) * FF_CHUNK] = (gate * jax.nn.sigmoid(gate) * val).astype(BF16)

    o_ref[0] = x + _dot(act_scr[...], wdown_ref[...])


def _const_spec(shape):
    nd = len(shape)
    return pl.BlockSpec(shape, lambda b, t: (0,) * nd, pipeline_mode=pl.Buffered(1))


def _attention_bias():
    i = np.arange(BLOCK)[:, None]
    c = np.arange(2 * BLOCK)[None, :]
    ok = (c > i) & (c <= i + BLOCK)
    general = np.where(ok, 0.0, NEG_BIG)
    first = np.where(ok & (c >= BLOCK), 0.0, NEG_BIG)
    return jnp.asarray(np.stack([general, first]), F32)


def _rope_constants():
    inv_freq = ROPE_THETA ** (-np.arange(0, ROPE_DIM, 2, dtype=np.float32) / ROPE_DIM)
    d = np.arange(LANES) % HEAD_DIM
    expand = np.zeros((4 * ROPE_HALF, 3 * LANES), np.float32)
    for f in range(ROPE_HALF):
        for part in (0, 2 * ROPE_HALF):
            expand[part + f, np.nonzero((d == f) | (d == f + ROPE_HALF))[0]] = 1.0
            expand[part + ROPE_HALF + f, LANES + np.nonzero(d == f + ROPE_HALF)[0]] = 1.0
            expand[part + ROPE_HALF + f, 2 * LANES + np.nonzero(d == f)[0]] = -1.0
    passlane = (d >= ROPE_DIM).astype(np.float32)[None, :]
    return (jnp.asarray(inv_freq[:, None], F32), jnp.asarray(expand, BF16), jnp.asarray(passlane, F32))


def _rope_gains(q_norm, k_norm):
    def rows(g):
        return [jnp.tile(g, LANES // HEAD_DIM), jnp.tile(jnp.roll(g, ROPE_HALF), LANES // HEAD_DIM),
                jnp.tile(jnp.roll(g, -ROPE_HALF), LANES // HEAD_DIM)]
    return jnp.stack(rows(q_norm * (HEAD_DIM ** -0.5 * LOG2E)) + rows(k_norm)).astype(F32)


def kernel(x, positions, attn_norm, w_in, b_gate, w_pool, pool_scale, q_norm, k_norm, sinks, w_out, ffn_norm,
           w_up, conv_w, conv_b, w_down):
    B, S, D = x.shape
    assert D == D_MODEL and S % MIXER_TILE == 0 and MIXER_TILE % BLOCK == 0 and S % FFN_TILE == 0
    assert attn_norm.shape[0] == 1, "single-layer block"
    ts = MIXER_TILE
    tf = FFN_TILE
    params = pltpu.CompilerParams(dimension_semantics=("arbitrary", "arbitrary"), vmem_limit_bytes=VMEM_LIMIT)

    invf_col, expand, passlane = _rope_constants()
    seg = np.arange(MXU_DIM) // HEAD_DIM
    block_mean = jnp.asarray((seg[:, None] == seg[None, :]) / HEAD_DIM, BF16)
    ident = jnp.asarray(np.tile(np.eye(BLOCK, dtype=np.float32), (GQA_GROUP // 2, 1)), BF16)
    n_stage = len(POOL_WINDOWS)

    const_shapes = ((1, D), (D, IN_WIDTH), (1, 2 * D), (4, POOL_GROUP, POOL_GROUP), (1, D), (6, LANES),
                    (ROPE_HALF, 1), (4 * ROPE_HALF, 3 * LANES), (1, LANES), (MXU_DIM, MXU_DIM),
                    (2, BLOCK, 2 * BLOCK), (GQA_GROUP // 2 * BLOCK, BLOCK), (D, D))
    x1 = pl.pallas_call(
        _mixer_kernel,
        grid_spec=pltpu.PrefetchScalarGridSpec(
            num_scalar_prefetch=1,
            grid=(B, S // ts),
            in_specs=[
                pl.BlockSpec((1, ts, D), lambda b, t, s: (b, t, 0)),
                pl.BlockSpec((1, 1, ts), lambda b, t, s: (b, 0, t)),
            ] + [pl.BlockSpec(shp, functools.partial(lambda nd, b, t, s: (0,) * nd, len(shp)),
                              pipeline_mode=pl.Buffered(1)) for shp in const_shapes],
            out_specs=pl.BlockSpec((1, ts, D), lambda b, t, s: (b, t, 0)),
            scratch_shapes=[
                pltpu.VMEM((n_stage, D // LANES, POOL_HALO + ts, LANES), F32),
                pltpu.VMEM((4, LANES, BLOCK), BF16),
                pltpu.VMEM((4, LANES, ts), BF16),
                pltpu.VMEM((4, BLOCK, 2 * LANES), BF16),
                pltpu.VMEM((4, ts, 2 * LANES), BF16),
                pltpu.VMEM((ts, Q_WIDTH), BF16),
                pltpu.VMEM((ts, Q_WIDTH), F32),
            ]),
        out_shape=jax.ShapeDtypeStruct((B, S, D), F32),
        compiler_params=params,
        name="mixer",
    )(sinks[0], x, positions.reshape(B, 1, S), attn_norm, w_in[0].astype(BF16), b_gate, w_pool[0].astype(BF16),
      pool_scale, _rope_gains(q_norm[0], k_norm[0]), invf_col, expand, passlane, block_mean, _attention_bias(),
      ident, w_out[0].astype(BF16))

    out = pl.pallas_call(
        _ffn_kernel,
        grid=(B, S // tf),
        in_specs=[pl.BlockSpec((1, tf, D), lambda b, t: (b, t, 0)), _const_spec((1, D)), _const_spec((D, 2 * D_FF)),
                  _const_spec((3, 2 * D_FF)), _const_spec((1, 2 * D_FF)), _const_spec((D_FF, D))],
        out_specs=pl.BlockSpec((1, tf, D), lambda b, t: (b, t, 0)),
        out_shape=jax.ShapeDtypeStruct((B, S, D), F32),
        scratch_shapes=[pltpu.VMEM((2 * D_FF // LANES, CONV_HALO + tf, LANES), F32), pltpu.VMEM((tf, D_FF), BF16)],
        compiler_params=params,
        name="ffn",
    )(x1, ffn_norm, w_up[0].astype(BF16), conv_w[0], conv_b, w_down[0].astype(BF16))
    return out
```

```python
import functools
import math

import numpy as np
import jax
import jax.numpy as jnp
from jax import lax
from jax.experimental import pallas as pl
from jax.experimental.pallas import tpu as pltpu

D_MODEL = 1024
POOL_WINDOWS = (2, 4, 8, 16)
POOL_GROUP = 256
POOL_HALO = 32
POOL_HEAD = 16
HEAD_DIM = 64
N_Q_HEADS = 16
N_KV_HEADS = 2
GQA_GROUP = 8
BLOCK = 128
ROPE_DIM = 16
ROPE_HALF = ROPE_DIM // 2
ROPE_THETA = 500000.0
Q_WIDTH = N_Q_HEADS * HEAD_DIM
KV_WIDTH = N_KV_HEADS * HEAD_DIM
D_FF = 2816
CONV_HALO = 8
EPS = 1e-6
IN_WIDTH = D_MODEL + Q_WIDTH + 2 * KV_WIDTH + 2 * D_MODEL
O_Q = D_MODEL
O_K = O_Q + Q_WIDTH
O_V = O_K + KV_WIDTH
O_G = O_V + KV_WIDTH

LANES = 128
SUBLANES = 8
MXU_DIM = 256
NEG_BIG = -1e30
LOG2E = math.log2(math.e)

MIXER_TILE = 512
FFN_TILE = 512
FF_CHUNK = 256
VMEM_LIMIT = 56 * 1024 * 1024

BF16 = jnp.bfloat16
F32 = jnp.float32


def _dot(a, b):
    return jnp.dot(a, b, preferred_element_type=F32)


def _dot_tn(a, b):
    return lax.dot_general(a, b, (((0,), (0,)), ((), ())), preferred_element_type=F32)


def _rmsnorm_rows(x, g):
    r = lax.rsqrt(jnp.mean(x * x, axis=-1, keepdims=True) + EPS)
    return x * r * g


def _rope(x, cos_t, sin_fwd, sin_bwd):
    return x * cos_t + pltpu.roll(x, ROPE_HALF, 1) * sin_fwd + pltpu.roll(x, LANES - ROPE_HALF, 1) * sin_bwd


def _mixer_kernel(sinks_ref, x_ref, pos_ref, g1_ref, win_ref, bg_ref, wpool_ref, ps_ref, gain_ref, invf_ref,
                  expand_ref, passlane_ref, bd_ref, bias_ref, ident_ref, wout_ref, o_ref,
                  pool_scr, kt_prev, kt_cur, v_prev, v_cur, q_scr, b_scr):
    ts = x_ref.shape[1]
    t = pl.program_id(1)
    pairs = GQA_GROUP // 2

    @pl.when(t > 0)
    def _():
        kt_prev[...] = kt_cur[:, :, ts - BLOCK:]
        v_prev[...] = v_cur[:, ts - BLOCK:, :]

    @pl.when(t == 0)
    def _():
        pool_scr[0, :, 0:POOL_HALO, :] = jnp.zeros((D_MODEL // LANES, POOL_HALO, LANES), F32)
        kt_prev[...] = jnp.zeros_like(kt_prev)
        v_prev[:, :, 0:LANES] = jnp.zeros((4, BLOCK, LANES), BF16)
        lane = lax.broadcasted_iota(jnp.int32, (BLOCK, LANES), 1)
        for par in range(2):
            ones = jnp.where((lane >= HEAD_DIM) == (par == 1), 1.0, 0.0).astype(BF16)
            for kvh in range(N_KV_HEADS):
                v_prev[2 * kvh + par, :, LANES:] = ones
                for j in range(ts // BLOCK):
                    v_cur[2 * kvh + par, j * BLOCK:(j + 1) * BLOCK, LANES:] = ones

    x = x_ref[0]
    h = _rmsnorm_rows(x, g1_ref[...]).astype(BF16)
    zq = _dot(h, win_ref[:, O_Q:O_K])
    zkv = _dot(h, win_ref[:, O_K:O_G])

    ang = invf_ref[...] * pos_ref[0].astype(F32)
    cs = jnp.concatenate([jnp.cos(ang), jnp.sin(ang)], axis=0)
    cs_hi = cs.astype(BF16)
    cs_lo = (cs - cs_hi.astype(F32)).astype(BF16)
    tables = _dot_tn(jnp.concatenate([cs_hi, cs_lo], axis=0), expand_ref[...])
    cos_t = tables[:, 0:LANES] + passlane_ref[...]
    sin_fwd = tables[:, LANES:2 * LANES]
    sin_bwd = tables[:, 2 * LANES:3 * LANES]
    q_tabs = (cos_t * gain_ref[0:1, :], sin_fwd * gain_ref[1:2, :], sin_bwd * gain_ref[2:3, :])
    k_tabs = (cos_t * gain_ref[3:4, :], sin_fwd * gain_ref[4:5, :], sin_bwd * gain_ref[5:6, :])

    for c in range(Q_WIDTH // MXU_DIM):
        qs = zq[:, c * MXU_DIM:(c + 1) * MXU_DIM]
        r = lax.rsqrt(_dot((qs * qs).astype(BF16), bd_ref[...]) + EPS)
        for hf in range(MXU_DIM // LANES):
            sl = slice(hf * LANES, (hf + 1) * LANES)
            lo = c * MXU_DIM + hf * LANES
            q_scr[:, lo:lo + LANES] = (_rope(qs[:, sl], *q_tabs) * r[:, sl]).astype(BF16)

    ks = zkv[:, 0:KV_WIDTH]
    rk = lax.rsqrt(_dot((ks * ks).astype(BF16), bd_ref[0:LANES, 0:LANES]) + EPS)
    kt = (_rope(ks, *k_tabs) * rk).T.astype(BF16)
    zero_half = jnp.zeros((HEAD_DIM, ts), BF16)
    for kvh in range(N_KV_HEADS):
        half = kt[kvh * HEAD_DIM:(kvh + 1) * HEAD_DIM]
        kt_cur[2 * kvh] = jnp.concatenate([half, zero_half], axis=0)
        kt_cur[2 * kvh + 1] = jnp.concatenate([zero_half, half], axis=0)
    vs = zkv[:, KV_WIDTH:]
    low = lax.broadcasted_iota(jnp.int32, (ts, LANES), 1) < HEAD_DIM
    v_swapped = pltpu.roll(vs, HEAD_DIM, 1)
    v_cur[0, :, 0:LANES] = jnp.where(low, vs, 0.0).astype(BF16)
    v_cur[1, :, 0:LANES] = jnp.where(low, 0.0, v_swapped).astype(BF16)
    v_cur[2, :, 0:LANES] = jnp.where(low, v_swapped, 0.0).astype(BF16)
    v_cur[3, :, 0:LANES] = jnp.where(low, 0.0, vs).astype(BF16)

    zu = _dot(h, win_ref[:, 0:O_Q])
    row_head = lax.broadcasted_iota(jnp.int32, (POOL_HEAD, LANES), 0) + t * ts
    a_parts = []
    for g, w in enumerate(POOL_WINDOWS):
        n_stage = g + 1
        pooled = []
        for s in range(POOL_GROUP // LANES):
            slab = g * (POOL_GROUP // LANES) + s
            u = zu[:, slab * LANES:(slab + 1) * LANES]
            pool_scr[0, slab, POOL_HALO:, :] = u
            for k in range(1, n_stage):
                sh = 2 ** (k - 1)
                lo = SUBLANES * k
                pool_scr[k, slab, lo:, :] = (pool_scr[k - 1, slab, lo:, :]
                                             + pool_scr[k - 1, slab, lo - sh:POOL_HALO + ts - sh, :])
            sh = w // 2
            wsum = (pool_scr[n_stage - 1, slab, POOL_HALO:, :]
                    + pool_scr[n_stage - 1, slab, POOL_HALO - sh:POOL_HALO + ts - sh, :])
            cnt = jnp.minimum(row_head + 1, w).astype(F32)
            head = wsum[:POOL_HEAD] / cnt
            tail = wsum[POOL_HEAD:] * (1.0 / w)
            pooled.append(jnp.concatenate([head, tail], axis=0) - u)
            pool_scr[0, slab, 0:POOL_HALO, :] = u[ts - POOL_HALO:]
        cols = slice(g * POOL_GROUP, (g + 1) * POOL_GROUP)
        mixed = _dot(jnp.concatenate(pooled, axis=1).astype(BF16), wpool_ref[g])
        a_parts.append(mixed * ps_ref[:, cols])

    gates = jax.nn.sigmoid(_dot(h, win_ref[:, O_G:]) + bg_ref[...])

    lane_low = lax.broadcasted_iota(jnp.int32, (BLOCK, LANES), 1) < HEAD_DIM
    for j in range(ts // BLOCK):
        if j == 0:
            bias = jnp.where(t == 0, bias_ref[1], bias_ref[0]).astype(BF16)
        else:
            bias = bias_ref[0].astype(BF16)
        rows = slice(j * BLOCK, (j + 1) * BLOCK)
        band = slice((j - 1) * BLOCK, (j + 1) * BLOCK)
        for kvh in range(N_KV_HEADS):
            base = kvh * GQA_GROUP * HEAD_DIM
            qstack = jnp.concatenate(
                [q_scr[rows, base + p * LANES:base + (p + 1) * LANES] for p in range(pairs)], axis=0)
            lhs = jnp.concatenate([qstack, ident_ref[...]], axis=1)
            acc = None
            maxes = []
            for par in range(2):
                var = 2 * kvh + par
                if j == 0:
                    k_band = jnp.concatenate([kt_prev[var], kt_cur[var, :, 0:BLOCK]], axis=1)
                    v_band = jnp.concatenate([v_prev[var], v_cur[var, 0:BLOCK, :]], axis=0)
                else:
                    k_band = kt_cur[var, :, band]
                    v_band = v_cur[var, band, :]
                s = _dot(lhs, jnp.concatenate([k_band, bias], axis=0))
                probs = []
                for p in range(pairs):
                    sp = s[p * BLOCK:(p + 1) * BLOCK]
                    m = jnp.max(sp, axis=-1, keepdims=True)
                    maxes.append(m)
                    probs.append(jnp.exp2(sp - m).astype(BF16))
                pv = _dot(jnp.concatenate(probs, axis=0), v_band)
                acc = pv if acc is None else acc + pv
            for p in range(pairs):
                blk = acc[p * BLOCK:(p + 1) * BLOCK]
                head = kvh * GQA_GROUP + 2 * p
                sink = jnp.where(lane_low[0:1], sinks_ref[head] * LOG2E, sinks_ref[head + 1] * LOG2E)
                denom = blk[:, LANES:] + jnp.exp2(sink - jnp.where(lane_low, maxes[p], maxes[pairs + p]))
                b_scr[rows, base + p * LANES:base + (p + 1) * LANES] = blk[:, :LANES] / denom

    a = jnp.concatenate(a_parts, axis=1)
    y = gates[:, :D_MODEL] * a + gates[:, D_MODEL:] * b_scr[...]
    o_ref[0] = x + _dot(y.astype(BF16), wout_ref[...])


def _ffn_kernel(x_ref, g2_ref, wup_ref, cw_ref, cb_ref, wdown_ref, o_ref, up_scr, act_scr):
    ts = x_ref.shape[1]
    t = pl.program_id(1)

    @pl.when(t == 0)
    def _():
        up_scr[:, 0:CONV_HALO, :] = jnp.zeros((up_scr.shape[0], CONV_HALO, LANES), F32)

    x = x_ref[0]
    h = _rmsnorm_rows(x, g2_ref[...]).astype(BF16)

    def conv(col0):
        up = _dot(h, wup_ref[:, col0:col0 + FF_CHUNK])
        parts = []
        for s in range(FF_CHUNK // LANES):
            slab = (col0 + s * LANES) // LANES
            cols = slice(col0 + s * LANES, col0 + (s + 1) * LANES)
            cur = up[:, s * LANES:(s + 1) * LANES]
            up_scr[slab, CONV_HALO:, :] = cur
            parts.append(cb_ref[:, cols] + cw_ref[2:3, cols] * cur
                         + cw_ref[1:2, cols] * up_scr[slab, CONV_HALO - 1:CONV_HALO - 1 + ts, :]
                         + cw_ref[0:1, cols] * up_scr[slab, CONV_HALO - 2:CONV_HALO - 2 + ts, :])
            up_scr[slab, 0:CONV_HALO, :] = cur[ts - CONV_HALO:]
        return jnp.concatenate(parts, axis=1)

    for c in range(D_FF // FF_CHUNK):
        gate = conv(c * FF_CHUNK)
        val = conv(D_FF + c * FF_CHUNK)
        act_scr[:, c * FF_CHUNK:(c + 1) * FF_CHUNK] = (gate * jax.nn.sigmoid(gate) * val).astype(BF16)

    o_ref[0] = x + _dot(act_scr[...], wdown_ref[...])


def _const_spec(shape):
    nd = len(shape)
    return pl.BlockSpec(shape, lambda b, t: (0,) * nd, pipeline_mode=pl.Buffered(1))


def _attention_bias():
    i = np.arange(BLOCK)[:, None]
    c = np.arange(2 * BLOCK)[None, :]
    ok = (c > i) & (c <= i + BLOCK)
    general = np.where(ok, 0.0, NEG_BIG)
    first = np.where(ok & (c >= BLOCK), 0.0, NEG_BIG)
    return jnp.asarray(np.stack([general, first]), F32)


def _rope_constants():
    inv_freq = ROPE_THETA ** (-np.arange(0, ROPE_DIM, 2, dtype=np.float32) / ROPE_DIM)
    d = np.arange(LANES) % HEAD_DIM
    expand = np.zeros((4 * ROPE_HALF, 3 * LANES), np.float32)
    for f in range(ROPE_HALF):
        for part in (0, 2 * ROPE_HALF):
            expand[part + f, np.nonzero((d == f) | (d == f + ROPE_HALF))[0]] = 1.0
            expand[part + ROPE_HALF + f, LANES + np.nonzero(d == f + ROPE_HALF)[0]] = 1.0
            expand[part + ROPE_HALF + f, 2 * LANES + np.nonzero(d == f)[0]] = -1.0
    passlane = (d >= ROPE_DIM).astype(np.float32)[None, :]
    return (jnp.asarray(inv_freq[:, None], F32), jnp.asarray(expand, BF16), jnp.asarray(passlane, F32))


def _rope_gains(q_norm, k_norm):
    def rows(g):
        return [jnp.tile(g, LANES // HEAD_DIM), jnp.tile(jnp.roll(g, ROPE_HALF), LANES // HEAD_DIM),
                jnp.tile(jnp.roll(g, -ROPE_HALF), LANES // HEAD_DIM)]
    return jnp.stack(rows(q_norm * (HEAD_DIM ** -0.5 * LOG2E)) + rows(k_norm)).astype(F32)


def kernel(x, positions, attn_norm, w_in, b_gate, w_pool, pool_scale, q_norm, k_norm, sinks, w_out, ffn_norm,
           w_up, conv_w, conv_b, w_down):
    B, S, D = x.shape
    assert D == D_MODEL and S % MIXER_TILE == 0 and MIXER_TILE % BLOCK == 0 and S % FFN_TILE == 0
    assert attn_norm.shape[0] == 1, "single-layer block"
    ts = MIXER_TILE
    tf = FFN_TILE
    params = pltpu.CompilerParams(dimension_semantics=("arbitrary", "arbitrary"), vmem_limit_bytes=VMEM_LIMIT)

    invf_col, expand, passlane = _rope_constants()
    seg = np.arange(MXU_DIM) // HEAD_DIM
    block_mean = jnp.asarray((seg[:, None] == seg[None, :]) / HEAD_DIM, BF16)
    ident = jnp.asarray(np.tile(np.eye(BLOCK, dtype=np.float32), (GQA_GROUP // 2, 1)), BF16)
    n_stage = len(POOL_WINDOWS)

    const_shapes = ((1, D), (D, IN_WIDTH), (1, 2 * D), (4, POOL_GROUP, POOL_GROUP), (1, D), (6, LANES),
                    (ROPE_HALF, 1), (4 * ROPE_HALF, 3 * LANES), (1, LANES), (MXU_DIM, MXU_DIM),
                    (2, BLOCK, 2 * BLOCK), (GQA_GROUP // 2 * BLOCK, BLOCK), (D, D))
    x1 = pl.pallas_call(
        _mixer_kernel,
        grid_spec=pltpu.PrefetchScalarGridSpec(
            num_scalar_prefetch=1,
            grid=(B, S // ts),
            in_specs=[
                pl.BlockSpec((1, ts, D), lambda b, t, s: (b, t, 0)),
                pl.BlockSpec((1, 1, ts), lambda b, t, s: (b, 0, t)),
            ] + [pl.BlockSpec(shp, functools.partial(lambda nd, b, t, s: (0,) * nd, len(shp)),
                              pipeline_mode=pl.Buffered(1)) for shp in const_shapes],
            out_specs=pl.BlockSpec((1, ts, D), lambda b, t, s: (b, t, 0)),
            scratch_shapes=[
                pltpu.VMEM((n_stage, D // LANES, POOL_HALO + ts, LANES), F32),
                pltpu.VMEM((4, LANES, BLOCK), BF16),
                pltpu.VMEM((4, LANES, ts), BF16),
                pltpu.VMEM((4, BLOCK, 2 * LANES), BF16),
                pltpu.VMEM((4, ts, 2 * LANES), BF16),
                pltpu.VMEM((ts, Q_WIDTH), BF16),
                pltpu.VMEM((ts, Q_WIDTH), F32),
            ]),
        out_shape=jax.ShapeDtypeStruct((B, S, D), F32),
        compiler_params=params,
        name="mixer",
    )(sinks[0], x, positions.reshape(B, 1, S), attn_norm, w_in[0].astype(BF16), b_gate, w_pool[0].astype(BF16),
      pool_scale, _rope_gains(q_norm[0], k_norm[0]), invf_col, expand, passlane, block_mean, _attention_bias(),
      ident, w_out[0].astype(BF16))

    out = pl.pallas_call(
        _ffn_kernel,
        grid=(B, S // tf),
        in_specs=[pl.BlockSpec((1, tf, D), lambda b, t: (b, t, 0)), _const_spec((1, D)), _const_spec((D, 2 * D_FF)),
                  _const_spec((3, 2 * D_FF)), _const_spec((1, 2 * D_FF)), _const_spec((D_FF, D))],
        out_specs=pl.BlockSpec((1, tf, D), lambda b, t: (b, t, 0)),
        out_shape=jax.ShapeDtypeStruct((B, S, D), F32),
        scratch_shapes=[pltpu.VMEM((2 * D_FF // LANES, CONV_HALO + tf, LANES), F32), pltpu.VMEM((tf, D_FF), BF16)],
        compiler_params=params,
        name="ffn",
    )(x1, ffn_norm, w_up[0].astype(BF16), conv_w[0], conv_b, w_down[0].astype(BF16))
    return out
```

```python
import functools
import math

import numpy as np
import jax
import jax.numpy as jnp
from jax import lax
from jax.experimental import pallas as pl
from jax.experimental.pallas import tpu as pltpu

D_MODEL = 1024
POOL_WINDOWS = (2, 4, 8, 16)
POOL_GROUP = 256
POOL_HALO = 32
POOL_HEAD = 16
HEAD_DIM = 64
N_Q_HEADS = 16
N_KV_HEADS = 2
GQA_GROUP = 8
BLOCK = 128
ROPE_DIM = 16
ROPE_HALF = ROPE_DIM // 2
ROPE_THETA = 500000.0
Q_WIDTH = N_Q_HEADS * HEAD_DIM
KV_WIDTH = N_KV_HEADS * HEAD_DIM
D_FF = 2816
CONV_HALO = 8
EPS = 1e-6
IN_WIDTH = D_MODEL + Q_WIDTH + 2 * KV_WIDTH + 2 * D_MODEL
O_Q = D_MODEL
O_K = O_Q + Q_WIDTH
O_V = O_K + KV_WIDTH
O_G = O_V + KV_WIDTH

LANES = 128
SUBLANES = 8
MXU_DIM = 256
NEG_BIG = -1e30
LOG2E = math.log2(math.e)

MIXER_TILE = 512
FFN_TILE = 512
FF_CHUNK = 256
VMEM_LIMIT = 56 * 1024 * 1024

BF16 = jnp.bfloat16
F32 = jnp.float32


def _dot(a, b):
    return jnp.dot(a, b, preferred_element_type=F32)


def _dot_tn(a, b):
    return lax.dot_general(a, b, (((0,), (0,)), ((), ())), preferred_element_type=F32)


def _rmsnorm_rows(x, g):
    r = lax.rsqrt(jnp.mean(x * x, axis=-1, keepdims=True) + EPS)
    return x * r * g


def _rope(x, cos_t, sin_fwd, sin_bwd):
    return x * cos_t + pltpu.roll(x, ROPE_HALF, 1) * sin_fwd + pltpu.roll(x, LANES - ROPE_HALF, 1) * sin_bwd


def _mixer_kernel(sinks_ref, x_ref, pos_ref, g1_ref, win_ref, bg_ref, wpool_ref, ps_ref, gain_ref, invf_ref,
                  expand_ref, passlane_ref, bd_ref, bias_ref, ident_ref, wout_ref, wup_f32_ref, wdown_f32_ref,
                  o_ref, wup_bf_ref, wdown_bf_ref,
                  pool_scr, kt_prev, kt_cur, v_prev, v_cur, q_scr, b_scr):
    ts = x_ref.shape[1]
    t = pl.program_id(1)
    pairs = GQA_GROUP // 2

    wup_bf_ref[...] = wup_f32_ref[...].astype(BF16)
    wdown_bf_ref[...] = wdown_f32_ref[...].astype(BF16)

    @pl.when(t > 0)
    def _():
        kt_prev[...] = kt_cur[:, :, ts - BLOCK:]
        v_prev[...] = v_cur[:, ts - BLOCK:, :]

    @pl.when(t == 0)
    def _():
        pool_scr[0, :, 0:POOL_HALO, :] = jnp.zeros((D_MODEL // LANES, POOL_HALO, LANES), F32)
        kt_prev[...] = jnp.zeros_like(kt_prev)
        v_prev[:, :, 0:LANES] = jnp.zeros((4, BLOCK, LANES), BF16)
        lane = lax.broadcasted_iota(jnp.int32, (BLOCK, LANES), 1)
        for par in range(2):
            ones = jnp.where((lane >= HEAD_DIM) == (par == 1), 1.0, 0.0).astype(BF16)
            for kvh in range(N_KV_HEADS):
                v_prev[2 * kvh + par, :, LANES:] = ones
                for j in range(ts // BLOCK):
                    v_cur[2 * kvh + par, j * BLOCK:(j + 1) * BLOCK, LANES:] = ones

    x = x_ref[0]
    h = _rmsnorm_rows(x, g1_ref[...]).astype(BF16)
    zq = _dot(h, win_ref[:, O_Q:O_K])
    zkv = _dot(h, win_ref[:, O_K:O_G])

    ang = invf_ref[...] * pos_ref[0].astype(F32)
    cs = jnp.concatenate([jnp.cos(ang), jnp.sin(ang)], axis=0)
    cs_hi = cs.astype(BF16)
    cs_lo = (cs - cs_hi.astype(F32)).astype(BF16)
    tables = _dot_tn(jnp.concatenate([cs_hi, cs_lo], axis=0), expand_ref[...])
    cos_t = tables[:, 0:LANES] + passlane_ref[...]
    sin_fwd = tables[:, LANES:2 * LANES]
    sin_bwd = tables[:, 2 * LANES:3 * LANES]
    q_tabs = (cos_t * gain_ref[0:1, :], sin_fwd * gain_ref[1:2, :], sin_bwd * gain_ref[2:3, :])
    k_tabs = (cos_t * gain_ref[3:4, :], sin_fwd * gain_ref[4:5, :], sin_bwd * gain_ref[5:6, :])

    for c in range(Q_WIDTH // MXU_DIM):
        qs = zq[:, c * MXU_DIM:(c + 1) * MXU_DIM]
        r = lax.rsqrt(_dot((qs * qs).astype(BF16), bd_ref[...]) + EPS)
        for hf in range(MXU_DIM // LANES):
            sl = slice(hf * LANES, (hf + 1) * LANES)
            lo = c * MXU_DIM + hf * LANES
            q_scr[:, lo:lo + LANES] = (_rope(qs[:, sl], *q_tabs) * r[:, sl]).astype(BF16)

    ks = zkv[:, 0:KV_WIDTH]
    rk = lax.rsqrt(_dot((ks * ks).astype(BF16), bd_ref[0:LANES, 0:LANES]) + EPS)
    kt = (_rope(ks, *k_tabs) * rk).T.astype(BF16)
    zero_half = jnp.zeros((HEAD_DIM, ts), BF16)
    for kvh in range(N_KV_HEADS):
        half = kt[kvh * HEAD_DIM:(kvh + 1) * HEAD_DIM]
        kt_cur[2 * kvh] = jnp.concatenate([half, zero_half], axis=0)
        kt_cur[2 * kvh + 1] = jnp.concatenate([zero_half, half], axis=0)
    vs = zkv[:, KV_WIDTH:]
    low = lax.broadcasted_iota(jnp.int32, (ts, LANES), 1) < HEAD_DIM
    v_swapped = pltpu.roll(vs, HEAD_DIM, 1)
    v_cur[0, :, 0:LANES] = jnp.where(low, vs, 0.0).astype(BF16)
    v_cur[1, :, 0:LANES] = jnp.where(low, 0.0, v_swapped).astype(BF16)
    v_cur[2, :, 0:LANES] = jnp.where(low, v_swapped, 0.0).astype(BF16)
    v_cur[3, :, 0:LANES] = jnp.where(low, 0.0, vs).astype(BF16)

    zu = _dot(h, win_ref[:, 0:O_Q])
    row_head = lax.broadcasted_iota(jnp.int32, (POOL_HEAD, LANES), 0) + t * ts
    a_parts = []
    for g, w in enumerate(POOL_WINDOWS):
        n_stage = g + 1
        pooled = []
        for s in range(POOL_GROUP // LANES):
            slab = g * (POOL_GROUP // LANES) + s
            u = zu[:, slab * LANES:(slab + 1) * LANES]
            pool_scr[0, slab, POOL_HALO:, :] = u
            for k in range(1, n_stage):
                sh = 2 ** (k - 1)
                lo = SUBLANES * k
                pool_scr[k, slab, lo:, :] = (pool_scr[k - 1, slab, lo:, :]
                                             + pool_scr[k - 1, slab, lo - sh:POOL_HALO + ts - sh, :])
            sh = w // 2
            wsum = (pool_scr[n_stage - 1, slab, POOL_HALO:, :]
                    + pool_scr[n_stage - 1, slab, POOL_HALO - sh:POOL_HALO + ts - sh, :])
            cnt = jnp.minimum(row_head + 1, w).astype(F32)
            head = wsum[:POOL_HEAD] / cnt
            tail = wsum[POOL_HEAD:] * (1.0 / w)
            pooled.append(jnp.concatenate([head, tail], axis=0) - u)
            pool_scr[0, slab, 0:POOL_HALO, :] = u[ts - POOL_HALO:]
        cols = slice(g * POOL_GROUP, (g + 1) * POOL_GROUP)
        mixed = _dot(jnp.concatenate(pooled, axis=1).astype(BF16), wpool_ref[g])
        a_parts.append(mixed * ps_ref[:, cols])

    gates = jax.nn.sigmoid(_dot(h, win_ref[:, O_G:]) + bg_ref[...])

    lane_low = lax.broadcasted_iota(jnp.int32, (BLOCK, LANES), 1) < HEAD_DIM
    for j in range(ts // BLOCK):
        if j == 0:
            bias = jnp.where(t == 0, bias_ref[1], bias_ref[0]).astype(BF16)
        else:
            bias = bias_ref[0].astype(BF16)
        rows = slice(j * BLOCK, (j + 1) * BLOCK)
        band = slice((j - 1) * BLOCK, (j + 1) * BLOCK)
        for kvh in range(N_KV_HEADS):
            base = kvh * GQA_GROUP * HEAD_DIM
            qstack = jnp.concatenate(
                [q_scr[rows, base + p * LANES:base + (p + 1) * LANES] for p in range(pairs)], axis=0)
            lhs = jnp.concatenate([qstack, ident_ref[...]], axis=1)
            acc = None
            maxes = []
            for par in range(2):
                var = 2 * kvh + par
                if j == 0:
                    k_band = jnp.concatenate([kt_prev[var], kt_cur[var, :, 0:BLOCK]], axis=1)
                    v_band = jnp.concatenate([v_prev[var], v_cur[var, 0:BLOCK, :]], axis=0)
                else:
                    k_band = kt_cur[var, :, band]
                    v_band = v_cur[var, band, :]
                s = _dot(lhs, jnp.concatenate([k_band, bias], axis=0))
                probs = []
                for p in range(pairs):
                    sp = s[p * BLOCK:(p + 1) * BLOCK]
                    m = jnp.max(sp, axis=-1, keepdims=True)
                    maxes.append(m)
                    probs.append(jnp.exp2(sp - m).astype(BF16))
                pv = _dot(jnp.concatenate(probs, axis=0), v_band)
                acc = pv if acc is None else acc + pv
            for p in range(pairs):
                blk = acc[p * BLOCK:(p + 1) * BLOCK]
                head = kvh * GQA_GROUP + 2 * p
                sink = jnp.where(lane_low[0:1], sinks_ref[head] * LOG2E, sinks_ref[head + 1] * LOG2E)
                denom = blk[:, LANES:] + jnp.exp2(sink - jnp.where(lane_low, maxes[p], maxes[pairs + p]))
                b_scr[rows, base + p * LANES:base + (p + 1) * LANES] = blk[:, :LANES] / denom

    a = jnp.concatenate(a_parts, axis=1)
    y = gates[:, :D_MODEL] * a + gates[:, D_MODEL:] * b_scr[...]
    o_ref[0] = x + _dot(y.astype(BF16), wout_ref[...])


def _ffn_kernel(x_ref, g2_ref, wup_ref, cw_ref, cb_ref, wdown_ref, o_ref, up_scr, act_scr):
    ts = x_ref.shape[1]
    t = pl.program_id(1)

    @pl.when(t == 0)
    def _():
        up_scr[:, 0:CONV_HALO, :] = jnp.zeros((up_scr.shape[0], CONV_HALO, LANES), F32)

    x = x_ref[0]
    h = _rmsnorm_rows(x, g2_ref[...]).astype(BF16)

    def conv(col0):
        up = _dot(h, wup_ref[:, col0:col0 + FF_CHUNK])
        parts = []
        for s in range(FF_CHUNK // LANES):
            slab = (col0 + s * LANES) // LANES
            cols = slice(col0 + s * LANES, col0 + (s + 1) * LANES)
            cur = up[:, s * LANES:(s + 1) * LANES]
            up_scr[slab, CONV_HALO:, :] = cur
            parts.append(cb_ref[:, cols] + cw_ref[2:3, cols] * cur
                         + cw_ref[1:2, cols] * up_scr[slab, CONV_HALO - 1:CONV_HALO - 1 + ts, :]
                         + cw_ref[0:1, cols] * up_scr[slab, CONV_HALO - 2:CONV_HALO - 2 + ts, :])
            up_scr[slab, 0:CONV_HALO, :] = cur[ts - CONV_HALO:]
        return jnp.concatenate(parts, axis=1)

    for c in range(D_FF // FF_CHUNK):
        gate = conv(c * FF_CHUNK)
        val = conv(D_FF + c * FF_CHUNK)
        act_scr[:, c * FF_CHUNK:(c + 1) * FF_CHUNK] = (gate * jax.nn.sigmoid(gate) * val).astype(BF16)

    o_ref[0] = x + _dot(act_scr[...], wdown_ref[...])


def _const_spec(shape):
    nd = len(shape)
    return pl.BlockSpec(shape, lambda b, t: (0,) * nd, pipeline_mode=pl.Buffered(1))


def _attention_bias():
    i = np.arange(BLOCK)[:, None]
    c = np.arange(2 * BLOCK)[None, :]
    ok = (c > i) & (c <= i + BLOCK)
    general = np.where(ok, 0.0, NEG_BIG)
    first = np.where(ok & (c >= BLOCK), 0.0, NEG_BIG)
    return jnp.asarray(np.stack([general, first]), F32)


def _rope_constants():
    inv_freq = ROPE_THETA ** (-np.arange(0, ROPE_DIM, 2, dtype=np.float32) / ROPE_DIM)
    d = np.arange(LANES) % HEAD_DIM
    expand = np.zeros((4 * ROPE_HALF, 3 * LANES), np.float32)
    for f in range(ROPE_HALF):
        for part in (0, 2 * ROPE_HALF):
            expand[part + f, np.nonzero((d == f) | (d == f + ROPE_HALF))[0]] = 1.0
            expand[part + ROPE_HALF + f, LANES + np.nonzero(d == f + ROPE_HALF)[0]] = 1.0
            expand[part + ROPE_HALF + f, 2 * LANES + np.nonzero(d == f)[0]] = -1.0
    passlane = (d >= ROPE_DIM).astype(np.float32)[None, :]
    return (jnp.asarray(inv_freq[:, None], F32), jnp.asarray(expand, BF16), jnp.asarray(passlane, F32))


def _rope_gains(q_norm, k_norm):
    def rows(g):
        return [jnp.tile(g, LANES // HEAD_DIM), jnp.tile(jnp.roll(g, ROPE_HALF), LANES // HEAD_DIM),
                jnp.tile(jnp.roll(g, -ROPE_HALF), LANES // HEAD_DIM)]
    return jnp.stack(rows(q_norm * (HEAD_DIM ** -0.5 * LOG2E)) + rows(k_norm)).astype(F32)


def kernel(x, positions, attn_norm, w_in, b_gate, w_pool, pool_scale, q_norm, k_norm, sinks, w_out, ffn_norm,
           w_up, conv_w, conv_b, w_down):
    B, S, D = x.shape
    assert D == D_MODEL and S % MIXER_TILE == 0 and MIXER_TILE % BLOCK == 0 and S % FFN_TILE == 0
    assert attn_norm.shape[0] == 1, "single-layer block"
    ts = MIXER_TILE
    tf = FFN_TILE
    params = pltpu.CompilerParams(dimension_semantics=("arbitrary", "arbitrary"), vmem_limit_bytes=VMEM_LIMIT)

    invf_col, expand, passlane = _rope_constants()
    seg = np.arange(MXU_DIM) // HEAD_DIM
    block_mean = jnp.asarray((seg[:, None] == seg[None, :]) / HEAD_DIM, BF16)
    ident = jnp.asarray(np.tile(np.eye(BLOCK, dtype=np.float32), (GQA_GROUP // 2, 1)), BF16)
    n_stage = len(POOL_WINDOWS)
    n_t = S // ts
    n_steps = B * n_t
    up_rows, down_rows = D // n_steps, 64
    assert D % n_steps == 0 and up_rows % 16 == 0 and D_FF % down_rows == 0 and D_FF // down_rows <= n_steps
    last_down = D_FF // down_rows - 1

    const_shapes = ((1, D), (D, IN_WIDTH), (1, 2 * D), (4, POOL_GROUP, POOL_GROUP), (1, D), (6, LANES),
                    (ROPE_HALF, 1), (4 * ROPE_HALF, 3 * LANES), (1, LANES), (MXU_DIM, MXU_DIM),
                    (2, BLOCK, 2 * BLOCK), (GQA_GROUP // 2 * BLOCK, BLOCK), (D, D))
    x1, w_up_bf, w_down_bf = pl.pallas_call(
        _mixer_kernel,
        grid_spec=pltpu.PrefetchScalarGridSpec(
            num_scalar_prefetch=1,
            grid=(B, S // ts),
            in_specs=[
                pl.BlockSpec((1, ts, D), lambda b, t, s: (b, t, 0)),
                pl.BlockSpec((1, 1, ts), lambda b, t, s: (b, 0, t)),
            ] + [pl.BlockSpec(shp, functools.partial(lambda nd, b, t, s: (0,) * nd, len(shp)),
                              pipeline_mode=pl.Buffered(1)) for shp in const_shapes] + [
                pl.BlockSpec((up_rows, 2 * D_FF), lambda b, t, s: (b * n_t + t, 0)),
                pl.BlockSpec((down_rows, D), lambda b, t, s: (jnp.minimum(b * n_t + t, last_down), 0)),
            ],
            out_specs=[
                pl.BlockSpec((1, ts, D), lambda b, t, s: (b, t, 0)),
                pl.BlockSpec((up_rows, 2 * D_FF), lambda b, t, s: (b * n_t + t, 0)),
                pl.BlockSpec((down_rows, D), lambda b, t, s: (jnp.minimum(b * n_t + t, last_down), 0)),
            ],
            scratch_shapes=[
                pltpu.VMEM((n_stage, D // LANES, POOL_HALO + ts, LANES), F32),
                pltpu.VMEM((4, LANES, BLOCK), BF16),
                pltpu.VMEM((4, LANES, ts), BF16),
                pltpu.VMEM((4, BLOCK, 2 * LANES), BF16),
                pltpu.VMEM((4, ts, 2 * LANES), BF16),
                pltpu.VMEM((ts, Q_WIDTH), BF16),
                pltpu.VMEM((ts, Q_WIDTH), F32),
            ]),
        out_shape=[jax.ShapeDtypeStruct((B, S, D), F32), jax.ShapeDtypeStruct((D, 2 * D_FF), BF16),
                   jax.ShapeDtypeStruct((D_FF, D), BF16)],
        compiler_params=params,
        name="mixer",
    )(sinks[0], x, positions.reshape(B, 1, S), attn_norm, w_in[0].astype(BF16), b_gate, w_pool[0].astype(BF16),
      pool_scale, _rope_gains(q_norm[0], k_norm[0]), invf_col, expand, passlane, block_mean, _attention_bias(),
      ident, w_out[0].astype(BF16), w_up[0], w_down[0])

    out = pl.pallas_call(
        _ffn_kernel,
        grid=(B, S // tf),
        in_specs=[pl.BlockSpec((1, tf, D), lambda b, t: (b, t, 0)), _const_spec((1, D)), _const_spec((D, 2 * D_FF)),
                  _const_spec((3, 2 * D_FF)), _const_spec((1, 2 * D_FF)), _const_spec((D_FF, D))],
        out_specs=pl.BlockSpec((1, tf, D), lambda b, t: (b, t, 0)),
        out_shape=jax.ShapeDtypeStruct((B, S, D), F32),
        scratch_shapes=[pltpu.VMEM((2 * D_FF // LANES, CONV_HALO + tf, LANES), F32), pltpu.VMEM((tf, D_FF), BF16)],
        compiler_params=params,
        name="ffn",
    )(x1, ffn_norm, w_up_bf, conv_w[0], conv_b, w_down_bf)
    return out
```

```python
import functools
import math

import numpy as np
import jax
import jax.numpy as jnp
from jax import lax
from jax.experimental import pallas as pl
from jax.experimental.pallas import tpu as pltpu

D_MODEL = 1024
POOL_WINDOWS = (2, 4, 8, 16)
POOL_GROUP = 256
POOL_HALO = 32
POOL_HEAD = 16
HEAD_DIM = 64
N_Q_HEADS = 16
N_KV_HEADS = 2
GQA_GROUP = 8
BLOCK = 128
ROPE_DIM = 16
ROPE_HALF = ROPE_DIM // 2
ROPE_THETA = 500000.0
Q_WIDTH = N_Q_HEADS * HEAD_DIM
KV_WIDTH = N_KV_HEADS * HEAD_DIM
D_FF = 2816
CONV_HALO = 8
EPS = 1e-6
IN_WIDTH = D_MODEL + Q_WIDTH + 2 * KV_WIDTH + 2 * D_MODEL
O_Q = D_MODEL
O_K = O_Q + Q_WIDTH
O_V = O_K + KV_WIDTH
O_G = O_V + KV_WIDTH

LANES = 128
SUBLANES = 8
MXU_DIM = 256
NEG_BIG = -1e30
LOG2E = math.log2(math.e)

MIXER_TILE = 512
FFN_TILE = 1024
RING_SLABS = 8
FF_CHUNK = 256
VMEM_LIMIT = 60 * 1024 * 1024

BF16 = jnp.bfloat16
F32 = jnp.float32


def _dot(a, b):
    return jnp.dot(a, b, preferred_element_type=F32)


def _dot_tn(a, b):
    return lax.dot_general(a, b, (((0,), (0,)), ((), ())), preferred_element_type=F32)


def _rmsnorm_rows(x, g):
    r = lax.rsqrt(jnp.mean(x * x, axis=-1, keepdims=True) + EPS)
    return x * r * g


def _rope(x, cos_t, sin_fwd, sin_bwd):
    return x * cos_t + pltpu.roll(x, ROPE_HALF, 1) * sin_fwd + pltpu.roll(x, LANES - ROPE_HALF, 1) * sin_bwd


def _mixer_kernel(sinks_ref, x_ref, pos_ref, g1_ref, win_ref, bg_ref, wpool_ref, ps_ref, gain_ref, invf_ref,
                  expand_ref, passlane_ref, bd_ref, bias_ref, ident_ref, wout_ref, wup_f32_ref, wdown_f32_ref,
                  o_ref, wup_bf_ref, wdown_bf_ref,
                  pool_scr, kt_prev, kt_cur, v_prev, v_cur, q_scr, b_scr):
    ts = x_ref.shape[1]
    t = pl.program_id(1)
    pairs = GQA_GROUP // 2

    wup_bf_ref[...] = wup_f32_ref[...].astype(BF16)
    wdown_bf_ref[...] = wdown_f32_ref[...].astype(BF16)

    @pl.when(t > 0)
    def _():
        kt_prev[...] = kt_cur[:, :, ts - BLOCK:]
        v_prev[...] = v_cur[:, ts - BLOCK:, :]

    @pl.when(t == 0)
    def _():
        pool_scr[0, :, 0:POOL_HALO, :] = jnp.zeros((D_MODEL // LANES, POOL_HALO, LANES), F32)
        kt_prev[...] = jnp.zeros_like(kt_prev)
        v_prev[:, :, 0:LANES] = jnp.zeros((4, BLOCK, LANES), BF16)
        lane = lax.broadcasted_iota(jnp.int32, (BLOCK, LANES), 1)
        for par in range(2):
            ones = jnp.where((lane >= HEAD_DIM) == (par == 1), 1.0, 0.0).astype(BF16)
            for kvh in range(N_KV_HEADS):
                v_prev[2 * kvh + par, :, LANES:] = ones
                for j in range(ts // BLOCK):
                    v_cur[2 * kvh + par, j * BLOCK:(j + 1) * BLOCK, LANES:] = ones

    x = x_ref[0]
    h = _rmsnorm_rows(x, g1_ref[...]).astype(BF16)
    zq = _dot(h, win_ref[:, O_Q:O_K])
    zkv = _dot(h, win_ref[:, O_K:O_G])

    ang = invf_ref[...] * pos_ref[0].astype(F32)
    cs = jnp.concatenate([jnp.cos(ang), jnp.sin(ang)], axis=0)
    cs_hi = cs.astype(BF16)
    cs_lo = (cs - cs_hi.astype(F32)).astype(BF16)
    tables = _dot_tn(jnp.concatenate([cs_hi, cs_lo], axis=0), expand_ref[...])
    cos_t = tables[:, 0:LANES] + passlane_ref[...]
    sin_fwd = tables[:, LANES:2 * LANES]
    sin_bwd = tables[:, 2 * LANES:3 * LANES]
    q_tabs = (cos_t * gain_ref[0:1, :], sin_fwd * gain_ref[1:2, :], sin_bwd * gain_ref[2:3, :])
    k_tabs = (cos_t * gain_ref[3:4, :], sin_fwd * gain_ref[4:5, :], sin_bwd * gain_ref[5:6, :])

    for c in range(Q_WIDTH // MXU_DIM):
        qs = zq[:, c * MXU_DIM:(c + 1) * MXU_DIM]
        r = lax.rsqrt(_dot((qs * qs).astype(BF16), bd_ref[...]) + EPS)
        for hf in range(MXU_DIM // LANES):
            sl = slice(hf * LANES, (hf + 1) * LANES)
            lo = c * MXU_DIM + hf * LANES
            q_scr[:, lo:lo + LANES] = (_rope(qs[:, sl], *q_tabs) * r[:, sl]).astype(BF16)

    ks = zkv[:, 0:KV_WIDTH]
    rk = lax.rsqrt(_dot((ks * ks).astype(BF16), bd_ref[0:LANES, 0:LANES]) + EPS)
    kt = (_rope(ks, *k_tabs) * rk).T.astype(BF16)
    zero_half = jnp.zeros((HEAD_DIM, ts), BF16)
    for kvh in range(N_KV_HEADS):
        half = kt[kvh * HEAD_DIM:(kvh + 1) * HEAD_DIM]
        kt_cur[2 * kvh] = jnp.concatenate([half, zero_half], axis=0)
        kt_cur[2 * kvh + 1] = jnp.concatenate([zero_half, half], axis=0)
    vs = zkv[:, KV_WIDTH:]
    low = lax.broadcasted_iota(jnp.int32, (ts, LANES), 1) < HEAD_DIM
    v_swapped = pltpu.roll(vs, HEAD_DIM, 1)
    v_cur[0, :, 0:LANES] = jnp.where(low, vs, 0.0).astype(BF16)
    v_cur[1, :, 0:LANES] = jnp.where(low, 0.0, v_swapped).astype(BF16)
    v_cur[2, :, 0:LANES] = jnp.where(low, v_swapped, 0.0).astype(BF16)
    v_cur[3, :, 0:LANES] = jnp.where(low, 0.0, vs).astype(BF16)

    zu = _dot(h, win_ref[:, 0:O_Q])
    row_head = lax.broadcasted_iota(jnp.int32, (POOL_HEAD, LANES), 0) + t * ts
    a_parts = []
    for g, w in enumerate(POOL_WINDOWS):
        n_stage = g + 1
        pooled = []
        for s in range(POOL_GROUP // LANES):
            slab = g * (POOL_GROUP // LANES) + s
            u = zu[:, slab * LANES:(slab + 1) * LANES]
            pool_scr[0, slab, POOL_HALO:, :] = u
            for k in range(1, n_stage):
                sh = 2 ** (k - 1)
                lo = SUBLANES * k
                pool_scr[k, slab, lo:, :] = (pool_scr[k - 1, slab, lo:, :]
                                             + pool_scr[k - 1, slab, lo - sh:POOL_HALO + ts - sh, :])
            sh = w // 2
            wsum = (pool_scr[n_stage - 1, slab, POOL_HALO:, :]
                    + pool_scr[n_stage - 1, slab, POOL_HALO - sh:POOL_HALO + ts - sh, :])
            cnt = jnp.minimum(row_head + 1, w).astype(F32)
            head = wsum[:POOL_HEAD] / cnt
            tail = wsum[POOL_HEAD:] * (1.0 / w)
            pooled.append(jnp.concatenate([head, tail], axis=0) - u)
            pool_scr[0, slab, 0:POOL_HALO, :] = u[ts - POOL_HALO:]
        cols = slice(g * POOL_GROUP, (g + 1) * POOL_GROUP)
        mixed = _dot(jnp.concatenate(pooled, axis=1).astype(BF16), wpool_ref[g])
        a_parts.append(mixed * ps_ref[:, cols])

    gates = jax.nn.sigmoid(_dot(h, win_ref[:, O_G:]) + bg_ref[...])

    lane_low = lax.broadcasted_iota(jnp.int32, (BLOCK, LANES), 1) < HEAD_DIM
    for j in range(ts // BLOCK):
        if j == 0:
            bias = jnp.where(t == 0, bias_ref[1], bias_ref[0]).astype(BF16)
        else:
            bias = bias_ref[0].astype(BF16)
        rows = slice(j * BLOCK, (j + 1) * BLOCK)
        band = slice((j - 1) * BLOCK, (j + 1) * BLOCK)
        for kvh in range(N_KV_HEADS):
            base = kvh * GQA_GROUP * HEAD_DIM
            qstack = jnp.concatenate(
                [q_scr[rows, base + p * LANES:base + (p + 1) * LANES] for p in range(pairs)], axis=0)
            lhs = jnp.concatenate([qstack, ident_ref[...]], axis=1)
            acc = None
            maxes = []
            for par in range(2):
                var = 2 * kvh + par
                if j == 0:
                    k_band = jnp.concatenate([kt_prev[var], kt_cur[var, :, 0:BLOCK]], axis=1)
                    v_band = jnp.concatenate([v_prev[var], v_cur[var, 0:BLOCK, :]], axis=0)
                else:
                    k_band = kt_cur[var, :, band]
                    v_band = v_cur[var, band, :]
                s = _dot(lhs, jnp.concatenate([k_band, bias], axis=0))
                probs = []
                for p in range(pairs):
                    sp = s[p * BLOCK:(p + 1) * BLOCK]
                    m = jnp.max(sp, axis=-1, keepdims=True)
                    maxes.append(m)
                    probs.append(jnp.exp2(sp - m).astype(BF16))
                pv = _dot(jnp.concatenate(probs, axis=0), v_band)
                acc = pv if acc is None else acc + pv
            for p in range(pairs):
                blk = acc[p * BLOCK:(p + 1) * BLOCK]
                head = kvh * GQA_GROUP + 2 * p
                sink = jnp.where(lane_low[0:1], sinks_ref[head] * LOG2E, sinks_ref[head + 1] * LOG2E)
                denom = blk[:, LANES:] + jnp.exp2(sink - jnp.where(lane_low, maxes[p], maxes[pairs + p]))
                b_scr[rows, base + p * LANES:base + (p + 1) * LANES] = blk[:, :LANES] / denom

    a = jnp.concatenate(a_parts, axis=1)
    y = gates[:, :D_MODEL] * a + gates[:, D_MODEL:] * b_scr[...]
    o_ref[0] = x + _dot(y.astype(BF16), wout_ref[...])


def _ffn_kernel(x_ref, g2_ref, wup_ref, cw_ref, cb_ref, wdown_ref, o_ref, carry, ring, act_scr):
    ts = x_ref.shape[1]
    t = pl.program_id(1)

    @pl.when(t == 0)
    def _():
        carry[...] = jnp.zeros_like(carry)

    x = x_ref[0]
    h = _rmsnorm_rows(x, g2_ref[...]).astype(BF16)

    per_call = FF_CHUNK // LANES

    def conv(col0, call):
        up = _dot(h, wup_ref[:, col0:col0 + FF_CHUNK])
        parts = []
        for s in range(per_call):
            slab = (col0 + s * LANES) // LANES
            r = (call * per_call + s) % RING_SLABS
            cols = slice(col0 + s * LANES, col0 + (s + 1) * LANES)
            cur = up[:, s * LANES:(s + 1) * LANES]
            ring[r, 0:CONV_HALO, :] = carry[slab]
            ring[r, CONV_HALO:, :] = cur
            parts.append(cb_ref[:, cols] + cw_ref[2:3, cols] * cur
                         + cw_ref[1:2, cols] * ring[r, CONV_HALO - 1:CONV_HALO - 1 + ts, :]
                         + cw_ref[0:1, cols] * ring[r, CONV_HALO - 2:CONV_HALO - 2 + ts, :])
            carry[slab] = cur[ts - CONV_HALO:]
        return jnp.concatenate(parts, axis=1)

    for c in range(D_FF // FF_CHUNK):
        gate = conv(c * FF_CHUNK, 2 * c)
        val = conv(D_FF + c * FF_CHUNK, 2 * c + 1)
        act_scr[:, c * FF_CHUNK:(c + 1) * FF_CHUNK] = (gate * jax.nn.sigmoid(gate) * val).astype(BF16)

    o_ref[0] = x + _dot(act_scr[...], wdown_ref[...])


def _const_spec(shape):
    nd = len(shape)
    return pl.BlockSpec(shape, lambda b, t: (0,) * nd, pipeline_mode=pl.Buffered(1))


def _attention_bias():
    i = np.arange(BLOCK)[:, None]
    c = np.arange(2 * BLOCK)[None, :]
    ok = (c > i) & (c <= i + BLOCK)
    general = np.where(ok, 0.0, NEG_BIG)
    first = np.where(ok & (c >= BLOCK), 0.0, NEG_BIG)
    return jnp.asarray(np.stack([general, first]), F32)


def _rope_constants():
    inv_freq = ROPE_THETA ** (-np.arange(0, ROPE_DIM, 2, dtype=np.float32) / ROPE_DIM)
    d = np.arange(LANES) % HEAD_DIM
    expand = np.zeros((4 * ROPE_HALF, 3 * LANES), np.float32)
    for f in range(ROPE_HALF):
        for part in (0, 2 * ROPE_HALF):
            expand[part + f, np.nonzero((d == f) | (d == f + ROPE_HALF))[0]] = 1.0
            expand[part + ROPE_HALF + f, LANES + np.nonzero(d == f + ROPE_HALF)[0]] = 1.0
            expand[part + ROPE_HALF + f, 2 * LANES + np.nonzero(d == f)[0]] = -1.0
    passlane = (d >= ROPE_DIM).astype(np.float32)[None, :]
    return (jnp.asarray(inv_freq[:, None], F32), jnp.asarray(expand, BF16), jnp.asarray(passlane, F32))


def _rope_gains(q_norm, k_norm):
    def rows(g):
        return [jnp.tile(g, LANES // HEAD_DIM), jnp.tile(jnp.roll(g, ROPE_HALF), LANES // HEAD_DIM),
                jnp.tile(jnp.roll(g, -ROPE_HALF), LANES // HEAD_DIM)]
    return jnp.stack(rows(q_norm * (HEAD_DIM ** -0.5 * LOG2E)) + rows(k_norm)).astype(F32)


def kernel(x, positions, attn_norm, w_in, b_gate, w_pool, pool_scale, q_norm, k_norm, sinks, w_out, ffn_norm,
           w_up, conv_w, conv_b, w_down):
    B, S, D = x.shape
    assert D == D_MODEL and S % MIXER_TILE == 0 and MIXER_TILE % BLOCK == 0 and S % FFN_TILE == 0
    assert attn_norm.shape[0] == 1, "single-layer block"
    ts = MIXER_TILE
    tf = FFN_TILE
    params = pltpu.CompilerParams(dimension_semantics=("arbitrary", "arbitrary"), vmem_limit_bytes=VMEM_LIMIT)

    invf_col, expand, passlane = _rope_constants()
    seg = np.arange(MXU_DIM) // HEAD_DIM
    block_mean = jnp.asarray((seg[:, None] == seg[None, :]) / HEAD_DIM, BF16)
    ident = jnp.asarray(np.tile(np.eye(BLOCK, dtype=np.float32), (GQA_GROUP // 2, 1)), BF16)
    n_stage = len(POOL_WINDOWS)
    n_t = S // ts
    n_steps = B * n_t
    up_rows, down_rows = D // n_steps, 64
    assert D % n_steps == 0 and up_rows % 16 == 0 and D_FF % down_rows == 0 and D_FF // down_rows <= n_steps
    last_down = D_FF // down_rows - 1

    const_shapes = ((1, D), (D, IN_WIDTH), (1, 2 * D), (4, POOL_GROUP, POOL_GROUP), (1, D), (6, LANES),
                    (ROPE_HALF, 1), (4 * ROPE_HALF, 3 * LANES), (1, LANES), (MXU_DIM, MXU_DIM),
                    (2, BLOCK, 2 * BLOCK), (GQA_GROUP // 2 * BLOCK, BLOCK), (D, D))
    x1, w_up_bf, w_down_bf = pl.pallas_call(
        _mixer_kernel,
        grid_spec=pltpu.PrefetchScalarGridSpec(
            num_scalar_prefetch=1,
            grid=(B, S // ts),
            in_specs=[
                pl.BlockSpec((1, ts, D), lambda b, t, s: (b, t, 0)),
                pl.BlockSpec((1, 1, ts), lambda b, t, s: (b, 0, t)),
            ] + [pl.BlockSpec(shp, functools.partial(lambda nd, b, t, s: (0,) * nd, len(shp)),
                              pipeline_mode=pl.Buffered(1)) for shp in const_shapes] + [
                pl.BlockSpec((up_rows, 2 * D_FF), lambda b, t, s: (b * n_t + t, 0)),
                pl.BlockSpec((down_rows, D), lambda b, t, s: (jnp.minimum(b * n_t + t, last_down), 0)),
            ],
            out_specs=[
                pl.BlockSpec((1, ts, D), lambda b, t, s: (b, t, 0)),
                pl.BlockSpec((up_rows, 2 * D_FF), lambda b, t, s: (b * n_t + t, 0)),
                pl.BlockSpec((down_rows, D), lambda b, t, s: (jnp.minimum(b * n_t + t, last_down), 0)),
            ],
            scratch_shapes=[
                pltpu.VMEM((n_stage, D // LANES, POOL_HALO + ts, LANES), F32),
                pltpu.VMEM((4, LANES, BLOCK), BF16),
                pltpu.VMEM((4, LANES, ts), BF16),
                pltpu.VMEM((4, BLOCK, 2 * LANES), BF16),
                pltpu.VMEM((4, ts, 2 * LANES), BF16),
                pltpu.VMEM((ts, Q_WIDTH), BF16),
                pltpu.VMEM((ts, Q_WIDTH), F32),
            ]),
        out_shape=[jax.ShapeDtypeStruct((B, S, D), F32), jax.ShapeDtypeStruct((D, 2 * D_FF), BF16),
                   jax.ShapeDtypeStruct((D_FF, D), BF16)],
        compiler_params=params,
        name="mixer",
    )(sinks[0], x, positions.reshape(B, 1, S), attn_norm, w_in[0].astype(BF16), b_gate, w_pool[0].astype(BF16),
      pool_scale, _rope_gains(q_norm[0], k_norm[0]), invf_col, expand, passlane, block_mean, _attention_bias(),
      ident, w_out[0].astype(BF16), w_up[0], w_down[0])

    out = pl.pallas_call(
        _ffn_kernel,
        grid=(B, S // tf),
        in_specs=[pl.BlockSpec((1, tf, D), lambda b, t: (b, t, 0)), _const_spec((1, D)), _const_spec((D, 2 * D_FF)),
                  _const_spec((3, 2 * D_FF)), _const_spec((1, 2 * D_FF)), _const_spec((D_FF, D))],
        out_specs=pl.BlockSpec((1, tf, D), lambda b, t: (b, t, 0)),
        out_shape=jax.ShapeDtypeStruct((B, S, D), F32),
        scratch_shapes=[pltpu.VMEM((2 * D_FF // LANES, CONV_HALO, LANES), F32),
                        pltpu.VMEM((RING_SLABS, CONV_HALO + tf, LANES), F32), pltpu.VMEM((tf, D_FF), BF16)],
        compiler_params=params,
        name="ffn",
    )(x1, ffn_norm, w_up_bf, conv_w[0], conv_b, w_down_bf)
    return out
```

```python
import functools
import math

import numpy as np
import jax
import jax.numpy as jnp
from jax import lax
from jax.experimental import pallas as pl
from jax.experimental.pallas import tpu as pltpu

D_MODEL = 1024
POOL_WINDOWS = (2, 4, 8, 16)
POOL_GROUP = 256
POOL_HALO = 32
POOL_HEAD = 16
HEAD_DIM = 64
N_Q_HEADS = 16
N_KV_HEADS = 2
GQA_GROUP = 8
BLOCK = 128
ROPE_DIM = 16
ROPE_HALF = ROPE_DIM // 2
ROPE_THETA = 500000.0
Q_WIDTH = N_Q_HEADS * HEAD_DIM
KV_WIDTH = N_KV_HEADS * HEAD_DIM
D_FF = 2816
CONV_HALO = 8
EPS = 1e-6
IN_WIDTH = D_MODEL + Q_WIDTH + 2 * KV_WIDTH + 2 * D_MODEL
O_Q = D_MODEL
O_K = O_Q + Q_WIDTH
O_V = O_K + KV_WIDTH
O_G = O_V + KV_WIDTH

LANES = 128
SUBLANES = 8
MXU_DIM = 256
NEG_BIG = -1e30
LOG2E = math.log2(math.e)

MIXER_TILE = 512
FFN_TILE = 1024
WEIGHT_ROWS = 128
RING_SLABS = 8
FF_CHUNK = 256
VMEM_LIMIT = 60 * 1024 * 1024

BF16 = jnp.bfloat16
F32 = jnp.float32


def _dot(a, b):
    return jnp.dot(a, b, preferred_element_type=F32)


def _dot_tn(a, b):
    return lax.dot_general(a, b, (((0,), (0,)), ((), ())), preferred_element_type=F32)


def _rmsnorm_rows(x, g):
    r = lax.rsqrt(jnp.mean(x * x, axis=-1, keepdims=True) + EPS)
    return x * r * g


def _rope(x, cos_t, sin_fwd, sin_bwd):
    return x * cos_t + pltpu.roll(x, ROPE_HALF, 1) * sin_fwd + pltpu.roll(x, LANES - ROPE_HALF, 1) * sin_bwd


def _convert_weights(jobs, stage, sem):
    def copy(k):
        src, _ = jobs[k]
        return pltpu.make_async_copy(src, stage.at[k % 2, :, 0:src.shape[-1]], sem.at[k % 2])

    copy(0).start()
    for k, (src, dst) in enumerate(jobs):
        if k + 1 < len(jobs):
            copy(k + 1).start()
        copy(k).wait()
        dst[...] = stage[k % 2, :, 0:src.shape[-1]].astype(BF16)


def _mixer_kernel(sinks_ref, x_ref, pos_ref, g1_ref, win_hbm, bg_ref, wpool_hbm, ps_ref, gain_ref, invf_ref,
                  expand_ref, passlane_ref, bd_ref, bias_ref, ident_ref, wout_hbm, wup_f32_ref, wdown_f32_ref,
                  o_ref, wup_bf_ref, wdown_bf_ref,
                  pool_scr, kt_prev, kt_cur, v_prev, v_cur, q_scr, b_scr, win_ref, wpool_ref, wout_ref, stage, sem):
    ts = x_ref.shape[1]
    t = pl.program_id(1)
    pairs = GQA_GROUP // 2

    @pl.when((pl.program_id(0) == 0) & (t == 0))
    def _():
        jobs = []
        for r in range(0, D_MODEL, WEIGHT_ROWS):
            jobs.append((win_hbm.at[pl.ds(r, WEIGHT_ROWS), :], win_ref.at[pl.ds(r, WEIGHT_ROWS), :]))
        for r in range(0, D_MODEL, WEIGHT_ROWS):
            jobs.append((wout_hbm.at[pl.ds(r, WEIGHT_ROWS), :], wout_ref.at[pl.ds(r, WEIGHT_ROWS), :]))
        for g in range(len(POOL_WINDOWS)):
            for r in range(0, POOL_GROUP, WEIGHT_ROWS):
                jobs.append((wpool_hbm.at[g, pl.ds(r, WEIGHT_ROWS), :], wpool_ref.at[g, pl.ds(r, WEIGHT_ROWS), :]))
        _convert_weights(jobs, stage, sem)

    wup_bf_ref[...] = wup_f32_ref[...].astype(BF16)
    wdown_bf_ref[...] = wdown_f32_ref[...].astype(BF16)

    @pl.when(t > 0)
    def _():
        kt_prev[...] = kt_cur[:, :, ts - BLOCK:]
        v_prev[...] = v_cur[:, ts - BLOCK:, :]

    @pl.when(t == 0)
    def _():
        pool_scr[0, :, 0:POOL_HALO, :] = jnp.zeros((D_MODEL // LANES, POOL_HALO, LANES), F32)
        kt_prev[...] = jnp.zeros_like(kt_prev)
        v_prev[:, :, 0:LANES] = jnp.zeros((4, BLOCK, LANES), BF16)
        lane = lax.broadcasted_iota(jnp.int32, (BLOCK, LANES), 1)
        for par in range(2):
            ones = jnp.where((lane >= HEAD_DIM) == (par == 1), 1.0, 0.0).astype(BF16)
            for kvh in range(N_KV_HEADS):
                v_prev[2 * kvh + par, :, LANES:] = ones
                for j in range(ts // BLOCK):
                    v_cur[2 * kvh + par, j * BLOCK:(j + 1) * BLOCK, LANES:] = ones

    x = x_ref[0]
    h = _rmsnorm_rows(x, g1_ref[...]).astype(BF16)
    zq = _dot(h, win_ref[:, O_Q:O_K])
    zkv = _dot(h, win_ref[:, O_K:O_G])

    ang = invf_ref[...] * pos_ref[0].astype(F32)
    cs = jnp.concatenate([jnp.cos(ang), jnp.sin(ang)], axis=0)
    cs_hi = cs.astype(BF16)
    cs_lo = (cs - cs_hi.astype(F32)).astype(BF16)
    tables = _dot_tn(jnp.concatenate([cs_hi, cs_lo], axis=0), expand_ref[...])
    cos_t = tables[:, 0:LANES] + passlane_ref[...]
    sin_fwd = tables[:, LANES:2 * LANES]
    sin_bwd = tables[:, 2 * LANES:3 * LANES]
    q_tabs = (cos_t * gain_ref[0:1, :], sin_fwd * gain_ref[1:2, :], sin_bwd * gain_ref[2:3, :])
    k_tabs = (cos_t * gain_ref[3:4, :], sin_fwd * gain_ref[4:5, :], sin_bwd * gain_ref[5:6, :])

    for c in range(Q_WIDTH // MXU_DIM):
        qs = zq[:, c * MXU_DIM:(c + 1) * MXU_DIM]
        r = lax.rsqrt(_dot((qs * qs).astype(BF16), bd_ref[...]) + EPS)
        for hf in range(MXU_DIM // LANES):
            sl = slice(hf * LANES, (hf + 1) * LANES)
            lo = c * MXU_DIM + hf * LANES
            q_scr[:, lo:lo + LANES] = (_rope(qs[:, sl], *q_tabs) * r[:, sl]).astype(BF16)

    ks = zkv[:, 0:KV_WIDTH]
    rk = lax.rsqrt(_dot((ks * ks).astype(BF16), bd_ref[0:LANES, 0:LANES]) + EPS)
    kt = (_rope(ks, *k_tabs) * rk).T.astype(BF16)
    zero_half = jnp.zeros((HEAD_DIM, ts), BF16)
    for kvh in range(N_KV_HEADS):
        half = kt[kvh * HEAD_DIM:(kvh + 1) * HEAD_DIM]
        kt_cur[2 * kvh] = jnp.concatenate([half, zero_half], axis=0)
        kt_cur[2 * kvh + 1] = jnp.concatenate([zero_half, half], axis=0)
    vs = zkv[:, KV_WIDTH:]
    low = lax.broadcasted_iota(jnp.int32, (ts, LANES), 1) < HEAD_DIM
    v_swapped = pltpu.roll(vs, HEAD_DIM, 1)
    v_cur[0, :, 0:LANES] = jnp.where(low, vs, 0.0).astype(BF16)
    v_cur[1, :, 0:LANES] = jnp.where(low, 0.0, v_swapped).astype(BF16)
    v_cur[2, :, 0:LANES] = jnp.where(low, v_swapped, 0.0).astype(BF16)
    v_cur[3, :, 0:LANES] = jnp.where(low, 0.0, vs).astype(BF16)

    zu = _dot(h, win_ref[:, 0:O_Q])
    row_head = lax.broadcasted_iota(jnp.int32, (POOL_HEAD, LANES), 0) + t * ts
    a_parts = []
    for g, w in enumerate(POOL_WINDOWS):
        n_stage = g + 1
        pooled = []
        for s in range(POOL_GROUP // LANES):
            slab = g * (POOL_GROUP // LANES) + s
            u = zu[:, slab * LANES:(slab + 1) * LANES]
            pool_scr[0, slab, POOL_HALO:, :] = u
            for k in range(1, n_stage):
                sh = 2 ** (k - 1)
                lo = SUBLANES * k
                pool_scr[k, slab, lo:, :] = (pool_scr[k - 1, slab, lo:, :]
                                             + pool_scr[k - 1, slab, lo - sh:POOL_HALO + ts - sh, :])
            sh = w // 2
            wsum = (pool_scr[n_stage - 1, slab, POOL_HALO:, :]
                    + pool_scr[n_stage - 1, slab, POOL_HALO - sh:POOL_HALO + ts - sh, :])
            cnt = jnp.minimum(row_head + 1, w).astype(F32)
            head = wsum[:POOL_HEAD] / cnt
            tail = wsum[POOL_HEAD:] * (1.0 / w)
            pooled.append(jnp.concatenate([head, tail], axis=0) - u)
            pool_scr[0, slab, 0:POOL_HALO, :] = u[ts - POOL_HALO:]
        cols = slice(g * POOL_GROUP, (g + 1) * POOL_GROUP)
        mixed = _dot(jnp.concatenate(pooled, axis=1).astype(BF16), wpool_ref[g])
        a_parts.append(mixed * ps_ref[:, cols])

    gates = jax.nn.sigmoid(_dot(h, win_ref[:, O_G:]) + bg_ref[...])

    lane_low = lax.broadcasted_iota(jnp.int32, (BLOCK, LANES), 1) < HEAD_DIM
    for j in range(ts // BLOCK):
        if j == 0:
            bias = jnp.where(t == 0, bias_ref[1], bias_ref[0]).astype(BF16)
        else:
            bias = bias_ref[0].astype(BF16)
        rows = slice(j * BLOCK, (j + 1) * BLOCK)
        band = slice((j - 1) * BLOCK, (j + 1) * BLOCK)
        for kvh in range(N_KV_HEADS):
            base = kvh * GQA_GROUP * HEAD_DIM
            qstack = jnp.concatenate(
                [q_scr[rows, base + p * LANES:base + (p + 1) * LANES] for p in range(pairs)], axis=0)
            lhs = jnp.concatenate([qstack, ident_ref[...]], axis=1)
            acc = None
            maxes = []
            for par in range(2):
                var = 2 * kvh + par
                if j == 0:
                    k_band = jnp.concatenate([kt_prev[var], kt_cur[var, :, 0:BLOCK]], axis=1)
                    v_band = jnp.concatenate([v_prev[var], v_cur[var, 0:BLOCK, :]], axis=0)
                else:
                    k_band = kt_cur[var, :, band]
                    v_band = v_cur[var, band, :]
                s = _dot(lhs, jnp.concatenate([k_band, bias], axis=0))
                probs = []
                for p in range(pairs):
                    sp = s[p * BLOCK:(p + 1) * BLOCK]
                    m = jnp.max(sp, axis=-1, keepdims=True)
                    maxes.append(m)
                    probs.append(jnp.exp2(sp - m).astype(BF16))
                pv = _dot(jnp.concatenate(probs, axis=0), v_band)
                acc = pv if acc is None else acc + pv
            for p in range(pairs):
                blk = acc[p * BLOCK:(p + 1) * BLOCK]
                head = kvh * GQA_GROUP + 2 * p
                sink = jnp.where(lane_low[0:1], sinks_ref[head] * LOG2E, sinks_ref[head + 1] * LOG2E)
                denom = blk[:, LANES:] + jnp.exp2(sink - jnp.where(lane_low, maxes[p], maxes[pairs + p]))
                b_scr[rows, base + p * LANES:base + (p + 1) * LANES] = blk[:, :LANES] / denom

    a = jnp.concatenate(a_parts, axis=1)
    y = gates[:, :D_MODEL] * a + gates[:, D_MODEL:] * b_scr[...]
    o_ref[0] = x + _dot(y.astype(BF16), wout_ref[...])


def _ffn_kernel(x_ref, g2_ref, wup_ref, cw_ref, cb_ref, wdown_ref, o_ref, carry, ring, act_scr):
    ts = x_ref.shape[1]
    t = pl.program_id(1)

    @pl.when(t == 0)
    def _():
        carry[...] = jnp.zeros_like(carry)

    x = x_ref[0]
    h = _rmsnorm_rows(x, g2_ref[...]).astype(BF16)

    per_call = FF_CHUNK // LANES

    def conv(col0, call):
        up = _dot(h, wup_ref[:, col0:col0 + FF_CHUNK])
        parts = []
        for s in range(per_call):
            slab = (col0 + s * LANES) // LANES
            r = (call * per_call + s) % RING_SLABS
            cols = slice(col0 + s * LANES, col0 + (s + 1) * LANES)
            cur = up[:, s * LANES:(s + 1) * LANES]
            ring[r, 0:CONV_HALO, :] = carry[slab]
            ring[r, CONV_HALO:, :] = cur
            parts.append(cb_ref[:, cols] + cw_ref[2:3, cols] * cur
                         + cw_ref[1:2, cols] * ring[r, CONV_HALO - 1:CONV_HALO - 1 + ts, :]
                         + cw_ref[0:1, cols] * ring[r, CONV_HALO - 2:CONV_HALO - 2 + ts, :])
            carry[slab] = cur[ts - CONV_HALO:]
        return jnp.concatenate(parts, axis=1)

    for c in range(D_FF // FF_CHUNK):
        gate = conv(c * FF_CHUNK, 2 * c)
        val = conv(D_FF + c * FF_CHUNK, 2 * c + 1)
        act_scr[:, c * FF_CHUNK:(c + 1) * FF_CHUNK] = (gate * jax.nn.sigmoid(gate) * val).astype(BF16)

    o_ref[0] = x + _dot(act_scr[...], wdown_ref[...])


def _const_spec(shape):
    nd = len(shape)
    return pl.BlockSpec(shape, lambda b, t: (0,) * nd, pipeline_mode=pl.Buffered(1))


def _attention_bias():
    i = np.arange(BLOCK)[:, None]
    c = np.arange(2 * BLOCK)[None, :]
    ok = (c > i) & (c <= i + BLOCK)
    general = np.where(ok, 0.0, NEG_BIG)
    first = np.where(ok & (c >= BLOCK), 0.0, NEG_BIG)
    return jnp.asarray(np.stack([general, first]), F32)


def _rope_constants():
    inv_freq = ROPE_THETA ** (-np.arange(0, ROPE_DIM, 2, dtype=np.float32) / ROPE_DIM)
    d = np.arange(LANES) % HEAD_DIM
    expand = np.zeros((4 * ROPE_HALF, 3 * LANES), np.float32)
    for f in range(ROPE_HALF):
        for part in (0, 2 * ROPE_HALF):
            expand[part + f, np.nonzero((d == f) | (d == f + ROPE_HALF))[0]] = 1.0
            expand[part + ROPE_HALF + f, LANES + np.nonzero(d == f + ROPE_HALF)[0]] = 1.0
            expand[part + ROPE_HALF + f, 2 * LANES + np.nonzero(d == f)[0]] = -1.0
    passlane = (d >= ROPE_DIM).astype(np.float32)[None, :]
    return (jnp.asarray(inv_freq[:, None], F32), jnp.asarray(expand, BF16), jnp.asarray(passlane, F32))


def _rope_gains(q_norm, k_norm):
    def rows(g):
        return [jnp.tile(g, LANES // HEAD_DIM), jnp.tile(jnp.roll(g, ROPE_HALF), LANES // HEAD_DIM),
                jnp.tile(jnp.roll(g, -ROPE_HALF), LANES // HEAD_DIM)]
    return jnp.stack(rows(q_norm * (HEAD_DIM ** -0.5 * LOG2E)) + rows(k_norm)).astype(F32)


def kernel(x, positions, attn_norm, w_in, b_gate, w_pool, pool_scale, q_norm, k_norm, sinks, w_out, ffn_norm,
           w_up, conv_w, conv_b, w_down):
    B, S, D = x.shape
    assert D == D_MODEL and S % MIXER_TILE == 0 and MIXER_TILE % BLOCK == 0 and S % FFN_TILE == 0
    assert attn_norm.shape[0] == 1, "single-layer block"
    ts = MIXER_TILE
    tf = FFN_TILE
    params = pltpu.CompilerParams(dimension_semantics=("arbitrary", "arbitrary"), vmem_limit_bytes=VMEM_LIMIT)

    invf_col, expand, passlane = _rope_constants()
    seg = np.arange(MXU_DIM) // HEAD_DIM
    block_mean = jnp.asarray((seg[:, None] == seg[None, :]) / HEAD_DIM, BF16)
    ident = jnp.asarray(np.tile(np.eye(BLOCK, dtype=np.float32), (GQA_GROUP // 2, 1)), BF16)
    n_stage = len(POOL_WINDOWS)
    n_t = S // ts
    n_steps = B * n_t
    up_rows, down_rows = D // n_steps, 64
    assert D % n_steps == 0 and up_rows % 16 == 0 and D_FF % down_rows == 0 and D_FF // down_rows <= n_steps
    last_down = D_FF // down_rows - 1

    const_shapes = ((1, D), (D, IN_WIDTH), (1, 2 * D), (4, POOL_GROUP, POOL_GROUP), (1, D), (6, LANES),
                    (ROPE_HALF, 1), (4 * ROPE_HALF, 3 * LANES), (1, LANES), (MXU_DIM, MXU_DIM),
                    (2, BLOCK, 2 * BLOCK), (GQA_GROUP // 2 * BLOCK, BLOCK), (D, D))
    hbm_shapes = ((D, IN_WIDTH), (4, POOL_GROUP, POOL_GROUP), (D, D))
    x1, w_up_bf, w_down_bf = pl.pallas_call(
        _mixer_kernel,
        grid_spec=pltpu.PrefetchScalarGridSpec(
            num_scalar_prefetch=1,
            grid=(B, S // ts),
            in_specs=[
                pl.BlockSpec((1, ts, D), lambda b, t, s: (b, t, 0)),
                pl.BlockSpec((1, 1, ts), lambda b, t, s: (b, 0, t)),
            ] + [pl.BlockSpec(memory_space=pl.ANY) if shp in hbm_shapes else
                 pl.BlockSpec(shp, functools.partial(lambda nd, b, t, s: (0,) * nd, len(shp)),
                              pipeline_mode=pl.Buffered(1)) for shp in const_shapes] + [
                pl.BlockSpec((up_rows, 2 * D_FF), lambda b, t, s: (b * n_t + t, 0)),
                pl.BlockSpec((down_rows, D), lambda b, t, s: (jnp.minimum(b * n_t + t, last_down), 0)),
            ],
            out_specs=[
                pl.BlockSpec((1, ts, D), lambda b, t, s: (b, t, 0)),
                pl.BlockSpec((up_rows, 2 * D_FF), lambda b, t, s: (b * n_t + t, 0)),
                pl.BlockSpec((down_rows, D), lambda b, t, s: (jnp.minimum(b * n_t + t, last_down), 0)),
            ],
            scratch_shapes=[
                pltpu.VMEM((n_stage, D // LANES, POOL_HALO + ts, LANES), F32),
                pltpu.VMEM((4, LANES, BLOCK), BF16),
                pltpu.VMEM((4, LANES, ts), BF16),
                pltpu.VMEM((4, BLOCK, 2 * LANES), BF16),
                pltpu.VMEM((4, ts, 2 * LANES), BF16),
                pltpu.VMEM((ts, Q_WIDTH), BF16),
                pltpu.VMEM((ts, Q_WIDTH), F32),
                pltpu.VMEM((D, IN_WIDTH), BF16),
                pltpu.VMEM((4, POOL_GROUP, POOL_GROUP), BF16),
                pltpu.VMEM((D, D), BF16),
                pltpu.VMEM((2, WEIGHT_ROWS, IN_WIDTH), F32),
                pltpu.SemaphoreType.DMA((2,)),
            ]),
        out_shape=[jax.ShapeDtypeStruct((B, S, D), F32), jax.ShapeDtypeStruct((D, 2 * D_FF), BF16),
                   jax.ShapeDtypeStruct((D_FF, D), BF16)],
        compiler_params=params,
        name="mixer",
    )(sinks[0], x, positions.reshape(B, 1, S), attn_norm, w_in[0], b_gate, w_pool[0],
      pool_scale, _rope_gains(q_norm[0], k_norm[0]), invf_col, expand, passlane, block_mean, _attention_bias(),
      ident, w_out[0], w_up[0], w_down[0])

    out = pl.pallas_call(
        _ffn_kernel,
        grid=(B, S // tf),
        in_specs=[pl.BlockSpec((1, tf, D), lambda b, t: (b, t, 0)), _const_spec((1, D)), _const_spec((D, 2 * D_FF)),
                  _const_spec((3, 2 * D_FF)), _const_spec((1, 2 * D_FF)), _const_spec((D_FF, D))],
        out_specs=pl.BlockSpec((1, tf, D), lambda b, t: (b, t, 0)),
        out_shape=jax.ShapeDtypeStruct((B, S, D), F32),
        scratch_shapes=[pltpu.VMEM((2 * D_FF // LANES, CONV_HALO, LANES), F32),
                        pltpu.VMEM((RING_SLABS, CONV_HALO + tf, LANES), F32), pltpu.VMEM((tf, D_FF), BF16)],
        compiler_params=params,
        name="ffn",
    )(x1, ffn_norm, w_up_bf, conv_w[0], conv_b, w_down_bf)
    return out
```

```python
import functools
import math

import numpy as np
import jax
import jax.numpy as jnp
from jax import lax
from jax.experimental import pallas as pl
from jax.experimental.pallas import tpu as pltpu

D_MODEL = 1024
POOL_WINDOWS = (2, 4, 8, 16)
POOL_GROUP = 256
POOL_HALO = 32
POOL_HEAD = 16
HEAD_DIM = 64
N_Q_HEADS = 16
N_KV_HEADS = 2
GQA_GROUP = 8
BLOCK = 128
ROPE_DIM = 16
ROPE_HALF = ROPE_DIM // 2
ROPE_THETA = 500000.0
Q_WIDTH = N_Q_HEADS * HEAD_DIM
KV_WIDTH = N_KV_HEADS * HEAD_DIM
D_FF = 2816
CONV_HALO = 8
EPS = 1e-6
IN_WIDTH = D_MODEL + Q_WIDTH + 2 * KV_WIDTH + 2 * D_MODEL
O_Q = D_MODEL
O_K = O_Q + Q_WIDTH
O_V = O_K + KV_WIDTH
O_G = O_V + KV_WIDTH

LANES = 128
SUBLANES = 8
MXU_DIM = 256
NEG_BIG = -1e30
LOG2E = math.log2(math.e)

MIXER_TILE = 512
FFN_TILE = 1024
WEIGHT_ROWS = 128
RING_SLABS = 8
FF_CHUNK = 256
VMEM_LIMIT = 60 * 1024 * 1024

BF16 = jnp.bfloat16
F32 = jnp.float32


def _dot(a, b):
    return jnp.dot(a, b, preferred_element_type=F32)


def _dot_tn(a, b):
    return lax.dot_general(a, b, (((0,), (0,)), ((), ())), preferred_element_type=F32)


def _rmsnorm_rows(x, g):
    r = lax.rsqrt(jnp.mean(x * x, axis=-1, keepdims=True) + EPS)
    return x * r * g


def _rope(x, cos_t, sin_fwd, sin_bwd):
    return x * cos_t + pltpu.roll(x, ROPE_HALF, 1) * sin_fwd + pltpu.roll(x, LANES - ROPE_HALF, 1) * sin_bwd


def _convert_weights(jobs, stage, sem):
    def copy(k):
        src, _ = jobs[k]
        return pltpu.make_async_copy(src, stage.at[k % 2, :, 0:src.shape[-1]], sem.at[k % 2])

    copy(0).start()
    for k, (src, dst) in enumerate(jobs):
        if k + 1 < len(jobs):
            copy(k + 1).start()
        copy(k).wait()
        dst[...] = stage[k % 2, :, 0:src.shape[-1]].astype(BF16)


def _mixer_kernel(sinks_ref, x_ref, pos_ref, g1_ref, win_hbm, bg_ref, wpool_hbm, ps_ref, gain_ref, invf_ref,
                  expand_ref, passlane_ref, bd_ref, bias_ref, ident_ref, wout_hbm, wup_f32_ref, wdown_f32_ref,
                  o_ref, wup_bf_ref, wdown_bf_ref,
                  pool_scr, kt_prev, kt_cur, v_prev, v_cur, q_scr, b_scr, win_ref, wpool_ref, wout_ref, stage, wout_f32, wpool_f32, sem):
    ts = x_ref.shape[1]
    t = pl.program_id(1)
    pairs = GQA_GROUP // 2

    @pl.when((pl.program_id(0) == 0) & (t == 0))
    def _():
        small = (pltpu.make_async_copy(wout_hbm, wout_f32, sem.at[2]),
                 pltpu.make_async_copy(wpool_hbm, wpool_f32, sem.at[3]))
        for cp in small:
            cp.start()
        jobs = []
        for r in range(0, D_MODEL, WEIGHT_ROWS):
            jobs.append((win_hbm.at[pl.ds(r, WEIGHT_ROWS), :], win_ref.at[pl.ds(r, WEIGHT_ROWS), :]))
        _convert_weights(jobs, stage, sem)
        small[0].wait()
        wout_ref[...] = wout_f32[...].astype(BF16)
        small[1].wait()
        wpool_ref[...] = wpool_f32[...].astype(BF16)

    wup_bf_ref[...] = wup_f32_ref[...].astype(BF16)
    wdown_bf_ref[...] = wdown_f32_ref[...].astype(BF16)

    @pl.when(t > 0)
    def _():
        kt_prev[...] = kt_cur[:, :, ts - BLOCK:]
        v_prev[...] = v_cur[:, ts - BLOCK:, :]

    @pl.when(t == 0)
    def _():
        pool_scr[0, :, 0:POOL_HALO, :] = jnp.zeros((D_MODEL // LANES, POOL_HALO, LANES), F32)
        kt_prev[...] = jnp.zeros_like(kt_prev)
        v_prev[:, :, 0:LANES] = jnp.zeros((4, BLOCK, LANES), BF16)
        lane = lax.broadcasted_iota(jnp.int32, (BLOCK, LANES), 1)
        for par in range(2):
            ones = jnp.where((lane >= HEAD_DIM) == (par == 1), 1.0, 0.0).astype(BF16)
            for kvh in range(N_KV_HEADS):
                v_prev[2 * kvh + par, :, LANES:] = ones
                for j in range(ts // BLOCK):
                    v_cur[2 * kvh + par, j * BLOCK:(j + 1) * BLOCK, LANES:] = ones

    x = x_ref[0]
    h = _rmsnorm_rows(x, g1_ref[...]).astype(BF16)
    zq = _dot(h, win_ref[:, O_Q:O_K])
    zkv = _dot(h, win_ref[:, O_K:O_G])

    ang = invf_ref[...] * pos_ref[0].astype(F32)
    cs = jnp.concatenate([jnp.cos(ang), jnp.sin(ang)], axis=0)
    cs_hi = cs.astype(BF16)
    cs_lo = (cs - cs_hi.astype(F32)).astype(BF16)
    tables = _dot_tn(jnp.concatenate([cs_hi, cs_lo], axis=0), expand_ref[...])
    cos_t = tables[:, 0:LANES] + passlane_ref[...]
    sin_fwd = tables[:, LANES:2 * LANES]
    sin_bwd = tables[:, 2 * LANES:3 * LANES]
    q_tabs = (cos_t * gain_ref[0:1, :], sin_fwd * gain_ref[1:2, :], sin_bwd * gain_ref[2:3, :])
    k_tabs = (cos_t * gain_ref[3:4, :], sin_fwd * gain_ref[4:5, :], sin_bwd * gain_ref[5:6, :])

    for c in range(Q_WIDTH // MXU_DIM):
        qs = zq[:, c * MXU_DIM:(c + 1) * MXU_DIM]
        r = lax.rsqrt(_dot((qs * qs).astype(BF16), bd_ref[...]) + EPS)
        for hf in range(MXU_DIM // LANES):
            sl = slice(hf * LANES, (hf + 1) * LANES)
            lo = c * MXU_DIM + hf * LANES
            q_scr[:, lo:lo + LANES] = (_rope(qs[:, sl], *q_tabs) * r[:, sl]).astype(BF16)

    ks = zkv[:, 0:KV_WIDTH]
    rk = lax.rsqrt(_dot((ks * ks).astype(BF16), bd_ref[0:LANES, 0:LANES]) + EPS)
    kt = (_rope(ks, *k_tabs) * rk).T.astype(BF16)
    zero_half = jnp.zeros((HEAD_DIM, ts), BF16)
    for kvh in range(N_KV_HEADS):
        half = kt[kvh * HEAD_DIM:(kvh + 1) * HEAD_DIM]
        kt_cur[2 * kvh] = jnp.concatenate([half, zero_half], axis=0)
        kt_cur[2 * kvh + 1] = jnp.concatenate([zero_half, half], axis=0)
    vs = zkv[:, KV_WIDTH:]
    low = lax.broadcasted_iota(jnp.int32, (ts, LANES), 1) < HEAD_DIM
    v_swapped = pltpu.roll(vs, HEAD_DIM, 1)
    v_cur[0, :, 0:LANES] = jnp.where(low, vs, 0.0).astype(BF16)
    v_cur[1, :, 0:LANES] = jnp.where(low, 0.0, v_swapped).astype(BF16)
    v_cur[2, :, 0:LANES] = jnp.where(low, v_swapped, 0.0).astype(BF16)
    v_cur[3, :, 0:LANES] = jnp.where(low, 0.0, vs).astype(BF16)

    zu = _dot(h, win_ref[:, 0:O_Q])
    row_head = lax.broadcasted_iota(jnp.int32, (POOL_HEAD, LANES), 0) + t * ts
    a_parts = []
    for g, w in enumerate(POOL_WINDOWS):
        n_stage = g + 1
        pooled = []
        for s in range(POOL_GROUP // LANES):
            slab = g * (POOL_GROUP // LANES) + s
            u = zu[:, slab * LANES:(slab + 1) * LANES]
            pool_scr[0, slab, POOL_HALO:, :] = u
            for k in range(1, n_stage):
                sh = 2 ** (k - 1)
                lo = SUBLANES * k
                pool_scr[k, slab, lo:, :] = (pool_scr[k - 1, slab, lo:, :]
                                             + pool_scr[k - 1, slab, lo - sh:POOL_HALO + ts - sh, :])
            sh = w // 2
            wsum = (pool_scr[n_stage - 1, slab, POOL_HALO:, :]
                    + pool_scr[n_stage - 1, slab, POOL_HALO - sh:POOL_HALO + ts - sh, :])
            cnt = jnp.minimum(row_head + 1, w).astype(F32)
            head = wsum[:POOL_HEAD] / cnt
            tail = wsum[POOL_HEAD:] * (1.0 / w)
            pooled.append(jnp.concatenate([head, tail], axis=0) - u)
            pool_scr[0, slab, 0:POOL_HALO, :] = u[ts - POOL_HALO:]
        cols = slice(g * POOL_GROUP, (g + 1) * POOL_GROUP)
        mixed = _dot(jnp.concatenate(pooled, axis=1).astype(BF16), wpool_ref[g])
        a_parts.append(mixed * ps_ref[:, cols])

    gates = jax.nn.sigmoid(_dot(h, win_ref[:, O_G:]) + bg_ref[...])

    lane_low = lax.broadcasted_iota(jnp.int32, (BLOCK, LANES), 1) < HEAD_DIM
    for j in range(ts // BLOCK):
        if j == 0:
            bias = jnp.where(t == 0, bias_ref[1], bias_ref[0]).astype(BF16)
        else:
            bias = bias_ref[0].astype(BF16)
        rows = slice(j * BLOCK, (j + 1) * BLOCK)
        band = slice((j - 1) * BLOCK, (j + 1) * BLOCK)
        for kvh in range(N_KV_HEADS):
            base = kvh * GQA_GROUP * HEAD_DIM
            qstack = jnp.concatenate(
                [q_scr[rows, base + p * LANES:base + (p + 1) * LANES] for p in range(pairs)], axis=0)
            lhs = jnp.concatenate([qstack, ident_ref[...]], axis=1)
            acc = None
            maxes = []
            for par in range(2):
                var = 2 * kvh + par
                if j == 0:
                    k_band = jnp.concatenate([kt_prev[var], kt_cur[var, :, 0:BLOCK]], axis=1)
                    v_band = jnp.concatenate([v_prev[var], v_cur[var, 0:BLOCK, :]], axis=0)
                else:
                    k_band = kt_cur[var, :, band]
                    v_band = v_cur[var, band, :]
                s = _dot(lhs, jnp.concatenate([k_band, bias], axis=0))
                probs = []
                for p in range(pairs):
                    sp = s[p * BLOCK:(p + 1) * BLOCK]
                    m = jnp.max(sp, axis=-1, keepdims=True)
                    maxes.append(m)
                    probs.append(jnp.exp2(sp - m).astype(BF16))
                pv = _dot(jnp.concatenate(probs, axis=0), v_band)
                acc = pv if acc is None else acc + pv
            for p in range(pairs):
                blk = acc[p * BLOCK:(p + 1) * BLOCK]
                head = kvh * GQA_GROUP + 2 * p
                sink = jnp.where(lane_low[0:1], sinks_ref[head] * LOG2E, sinks_ref[head + 1] * LOG2E)
                denom = blk[:, LANES:] + jnp.exp2(sink - jnp.where(lane_low, maxes[p], maxes[pairs + p]))
                b_scr[rows, base + p * LANES:base + (p + 1) * LANES] = blk[:, :LANES] / denom

    a = jnp.concatenate(a_parts, axis=1)
    y = gates[:, :D_MODEL] * a + gates[:, D_MODEL:] * b_scr[...]
    o_ref[0] = x + _dot(y.astype(BF16), wout_ref[...])


def _ffn_kernel(x_ref, g2_ref, wup_ref, cw_ref, cb_ref, wdown_ref, o_ref, carry, ring, act_scr):
    ts = x_ref.shape[1]
    t = pl.program_id(1)

    @pl.when(t == 0)
    def _():
        carry[...] = jnp.zeros_like(carry)

    x = x_ref[0]
    h = _rmsnorm_rows(x, g2_ref[...]).astype(BF16)

    per_call = FF_CHUNK // LANES

    def conv(col0, call):
        up = _dot(h, wup_ref[:, col0:col0 + FF_CHUNK])
        parts = []
        for s in range(per_call):
            slab = (col0 + s * LANES) // LANES
            r = (call * per_call + s) % RING_SLABS
            cols = slice(col0 + s * LANES, col0 + (s + 1) * LANES)
            cur = up[:, s * LANES:(s + 1) * LANES]
            ring[r, 0:CONV_HALO, :] = carry[slab]
            ring[r, CONV_HALO:, :] = cur
            parts.append(cb_ref[:, cols] + cw_ref[2:3, cols] * cur
                         + cw_ref[1:2, cols] * ring[r, CONV_HALO - 1:CONV_HALO - 1 + ts, :]
                         + cw_ref[0:1, cols] * ring[r, CONV_HALO - 2:CONV_HALO - 2 + ts, :])
            carry[slab] = cur[ts - CONV_HALO:]
        return jnp.concatenate(parts, axis=1)

    for c in range(D_FF // FF_CHUNK):
        gate = conv(c * FF_CHUNK, 2 * c)
        val = conv(D_FF + c * FF_CHUNK, 2 * c + 1)
        act_scr[:, c * FF_CHUNK:(c + 1) * FF_CHUNK] = (gate * jax.nn.sigmoid(gate) * val).astype(BF16)

    o_ref[0] = x + _dot(act_scr[...], wdown_ref[...])


def _const_spec(shape):
    nd = len(shape)
    return pl.BlockSpec(shape, lambda b, t: (0,) * nd, pipeline_mode=pl.Buffered(1))


def _attention_bias():
    i = np.arange(BLOCK)[:, None]
    c = np.arange(2 * BLOCK)[None, :]
    ok = (c > i) & (c <= i + BLOCK)
    general = np.where(ok, 0.0, NEG_BIG)
    first = np.where(ok & (c >= BLOCK), 0.0, NEG_BIG)
    return jnp.asarray(np.stack([general, first]), F32)


def _rope_constants():
    inv_freq = ROPE_THETA ** (-np.arange(0, ROPE_DIM, 2, dtype=np.float32) / ROPE_DIM)
    d = np.arange(LANES) % HEAD_DIM
    expand = np.zeros((4 * ROPE_HALF, 3 * LANES), np.float32)
    for f in range(ROPE_HALF):
        for part in (0, 2 * ROPE_HALF):
            expand[part + f, np.nonzero((d == f) | (d == f + ROPE_HALF))[0]] = 1.0
            expand[part + ROPE_HALF + f, LANES + np.nonzero(d == f + ROPE_HALF)[0]] = 1.0
            expand[part + ROPE_HALF + f, 2 * LANES + np.nonzero(d == f)[0]] = -1.0
    passlane = (d >= ROPE_DIM).astype(np.float32)[None, :]
    return (jnp.asarray(inv_freq[:, None], F32), jnp.asarray(expand, BF16), jnp.asarray(passlane, F32))


def _rope_gains(q_norm, k_norm):
    def rows(g):
        return [jnp.tile(g, LANES // HEAD_DIM), jnp.tile(jnp.roll(g, ROPE_HALF), LANES // HEAD_DIM),
                jnp.tile(jnp.roll(g, -ROPE_HALF), LANES // HEAD_DIM)]
    return jnp.stack(rows(q_norm * (HEAD_DIM ** -0.5 * LOG2E)) + rows(k_norm)).astype(F32)


def kernel(x, positions, attn_norm, w_in, b_gate, w_pool, pool_scale, q_norm, k_norm, sinks, w_out, ffn_norm,
           w_up, conv_w, conv_b, w_down):
    B, S, D = x.shape
    assert D == D_MODEL and S % MIXER_TILE == 0 and MIXER_TILE % BLOCK == 0 and S % FFN_TILE == 0
    assert attn_norm.shape[0] == 1, "single-layer block"
    ts = MIXER_TILE
    tf = FFN_TILE
    params = pltpu.CompilerParams(dimension_semantics=("arbitrary", "arbitrary"), vmem_limit_bytes=VMEM_LIMIT)

    invf_col, expand, passlane = _rope_constants()
    seg = np.arange(MXU_DIM) // HEAD_DIM
    block_mean = jnp.asarray((seg[:, None] == seg[None, :]) / HEAD_DIM, BF16)
    ident = jnp.asarray(np.tile(np.eye(BLOCK, dtype=np.float32), (GQA_GROUP // 2, 1)), BF16)
    n_stage = len(POOL_WINDOWS)
    n_t = S // ts
    n_steps = B * n_t
    up_rows, down_rows = D // n_steps, 64
    assert D % n_steps == 0 and up_rows % 16 == 0 and D_FF % down_rows == 0 and D_FF // down_rows <= n_steps
    last_down = D_FF // down_rows - 1

    const_shapes = ((1, D), (D, IN_WIDTH), (1, 2 * D), (4, POOL_GROUP, POOL_GROUP), (1, D), (6, LANES),
                    (ROPE_HALF, 1), (4 * ROPE_HALF, 3 * LANES), (1, LANES), (MXU_DIM, MXU_DIM),
                    (2, BLOCK, 2 * BLOCK), (GQA_GROUP // 2 * BLOCK, BLOCK), (D, D))
    hbm_shapes = ((D, IN_WIDTH), (4, POOL_GROUP, POOL_GROUP), (D, D))
    x1, w_up_bf, w_down_bf = pl.pallas_call(
        _mixer_kernel,
        grid_spec=pltpu.PrefetchScalarGridSpec(
            num_scalar_prefetch=1,
            grid=(B, S // ts),
            in_specs=[
                pl.BlockSpec((1, ts, D), lambda b, t, s: (b, t, 0)),
                pl.BlockSpec((1, 1, ts), lambda b, t, s: (b, 0, t)),
            ] + [pl.BlockSpec(memory_space=pl.ANY) if shp in hbm_shapes else
                 pl.BlockSpec(shp, functools.partial(lambda nd, b, t, s: (0,) * nd, len(shp)),
                              pipeline_mode=pl.Buffered(1)) for shp in const_shapes] + [
                pl.BlockSpec((up_rows, 2 * D_FF), lambda b, t, s: (b * n_t + t, 0)),
                pl.BlockSpec((down_rows, D), lambda b, t, s: (jnp.minimum(b * n_t + t, last_down), 0)),
            ],
            out_specs=[
                pl.BlockSpec((1, ts, D), lambda b, t, s: (b, t, 0)),
                pl.BlockSpec((up_rows, 2 * D_FF), lambda b, t, s: (b * n_t + t, 0)),
                pl.BlockSpec((down_rows, D), lambda b, t, s: (jnp.minimum(b * n_t + t, last_down), 0)),
            ],
            scratch_shapes=[
                pltpu.VMEM((n_stage, D // LANES, POOL_HALO + ts, LANES), F32),
                pltpu.VMEM((4, LANES, BLOCK), BF16),
                pltpu.VMEM((4, LANES, ts), BF16),
                pltpu.VMEM((4, BLOCK, 2 * LANES), BF16),
                pltpu.VMEM((4, ts, 2 * LANES), BF16),
                pltpu.VMEM((ts, Q_WIDTH), BF16),
                pltpu.VMEM((ts, Q_WIDTH), F32),
                pltpu.VMEM((D, IN_WIDTH), BF16),
                pltpu.VMEM((4, POOL_GROUP, POOL_GROUP), BF16),
                pltpu.VMEM((D, D), BF16),
                pltpu.VMEM((2, WEIGHT_ROWS, IN_WIDTH), F32),
                pltpu.VMEM((D, D), F32),
                pltpu.VMEM((4, POOL_GROUP, POOL_GROUP), F32),
                pltpu.SemaphoreType.DMA((4,)),
            ]),
        out_shape=[jax.ShapeDtypeStruct((B, S, D), F32), jax.ShapeDtypeStruct((D, 2 * D_FF), BF16),
                   jax.ShapeDtypeStruct((D_FF, D), BF16)],
        compiler_params=params,
        name="mixer",
    )(sinks[0], x, positions.reshape(B, 1, S), attn_norm, w_in[0], b_gate, w_pool[0],
      pool_scale, _rope_gains(q_norm[0], k_norm[0]), invf_col, expand, passlane, block_mean, _attention_bias(),
      ident, w_out[0], w_up[0], w_down[0])

    out = pl.pallas_call(
        _ffn_kernel,
        grid=(B, S // tf),
        in_specs=[pl.BlockSpec((1, tf, D), lambda b, t: (b, t, 0)), _const_spec((1, D)), _const_spec((D, 2 * D_FF)),
                  _const_spec((3, 2 * D_FF)), _const_spec((1, 2 * D_FF)), _const_spec((D_FF, D))],
        out_specs=pl.BlockSpec((1, tf, D), lambda b, t: (b, t, 0)),
        out_shape=jax.ShapeDtypeStruct((B, S, D), F32),
        scratch_shapes=[pltpu.VMEM((2 * D_FF // LANES, CONV_HALO, LANES), F32),
                        pltpu.VMEM((RING_SLABS, CONV_HALO + tf, LANES), F32), pltpu.VMEM((tf, D_FF), BF16)],
        compiler_params=params,
        name="ffn",
    )(x1, ffn_norm, w_up_bf, conv_w[0], conv_b, w_down_bf)
    return out
```

```python
import functools
import math

import numpy as np
import jax
import jax.numpy as jnp
from jax import lax
from jax.experimental import pallas as pl
from jax.experimental.pallas import tpu as pltpu

D_MODEL = 1024
POOL_WINDOWS = (2, 4, 8, 16)
POOL_GROUP = 256
POOL_HALO = 32
POOL_HEAD = 16
HEAD_DIM = 64
N_Q_HEADS = 16
N_KV_HEADS = 2
GQA_GROUP = 8
BLOCK = 128
ROPE_DIM = 16
ROPE_HALF = ROPE_DIM // 2
ROPE_THETA = 500000.0
Q_WIDTH = N_Q_HEADS * HEAD_DIM
KV_WIDTH = N_KV_HEADS * HEAD_DIM
D_FF = 2816
CONV_HALO = 8
EPS = 1e-6
IN_WIDTH = D_MODEL + Q_WIDTH + 2 * KV_WIDTH + 2 * D_MODEL
O_Q = D_MODEL
O_K = O_Q + Q_WIDTH
O_V = O_K + KV_WIDTH
O_G = O_V + KV_WIDTH

LANES = 128
SUBLANES = 8
MXU_DIM = 256
NEG_BIG = -1e30
LOG2E = math.log2(math.e)

MIXER_TILE = 512
FFN_TILE = 1024
WEIGHT_ROWS = 128
STAGE_SLOTS = 4
RING_SLABS = 8
FF_CHUNK = 256
VMEM_LIMIT = 60 * 1024 * 1024

BF16 = jnp.bfloat16
F32 = jnp.float32


def _dot(a, b):
    return jnp.dot(a, b, preferred_element_type=F32)


def _dot_tn(a, b):
    return lax.dot_general(a, b, (((0,), (0,)), ((), ())), preferred_element_type=F32)


def _rmsnorm_rows(x, g):
    r = lax.rsqrt(jnp.mean(x * x, axis=-1, keepdims=True) + EPS)
    return x * r * g


def _rope(x, cos_t, sin_fwd, sin_bwd):
    return x * cos_t + pltpu.roll(x, ROPE_HALF, 1) * sin_fwd + pltpu.roll(x, LANES - ROPE_HALF, 1) * sin_bwd


def _convert_weights(jobs, stage, sem):
    slots = stage.shape[0]

    def copy(k):
        src, _ = jobs[k]
        return pltpu.make_async_copy(src, stage.at[k % slots, :, 0:src.shape[-1]], sem.at[k % slots])

    for k in range(min(slots - 1, len(jobs))):
        copy(k).start()
    for k, (src, dst) in enumerate(jobs):
        if k + slots - 1 < len(jobs):
            copy(k + slots - 1).start()
        copy(k).wait()
        dst[...] = stage[k % slots, :, 0:src.shape[-1]].astype(BF16)


def _mixer_kernel(sinks_ref, x_ref, pos_ref, g1_ref, win_hbm, bg_ref, wpool_hbm, ps_ref, gain_ref, invf_ref,
                  expand_ref, passlane_ref, bd_ref, bias_ref, ident_ref, wout_hbm, wup_f32_ref, wdown_f32_ref,
                  o_ref, wup_bf_ref, wdown_bf_ref,
                  pool_scr, kt_prev, kt_cur, v_prev, v_cur, q_scr, b_scr, win_ref, wpool_ref, wout_ref, stage, wout_f32, wpool_f32, sem):
    ts = x_ref.shape[1]
    t = pl.program_id(1)
    pairs = GQA_GROUP // 2

    @pl.when((pl.program_id(0) == 0) & (t == 0))
    def _():
        small = (pltpu.make_async_copy(wout_hbm, wout_f32, sem.at[STAGE_SLOTS]),
                 pltpu.make_async_copy(wpool_hbm, wpool_f32, sem.at[STAGE_SLOTS + 1]))
        for cp in small:
            cp.start()
        jobs = []
        for r in range(0, D_MODEL, WEIGHT_ROWS):
            jobs.append((win_hbm.at[pl.ds(r, WEIGHT_ROWS), :], win_ref.at[pl.ds(r, WEIGHT_ROWS), :]))
        _convert_weights(jobs, stage, sem)
        small[0].wait()
        wout_ref[...] = wout_f32[...].astype(BF16)
        small[1].wait()
        wpool_ref[...] = wpool_f32[...].astype(BF16)

    wup_bf_ref[...] = wup_f32_ref[...].astype(BF16)
    wdown_bf_ref[...] = wdown_f32_ref[...].astype(BF16)

    @pl.when(t > 0)
    def _():
        kt_prev[...] = kt_cur[:, :, ts - BLOCK:]
        v_prev[...] = v_cur[:, ts - BLOCK:, :]

    @pl.when(t == 0)
    def _():
        pool_scr[0, :, 0:POOL_HALO, :] = jnp.zeros((D_MODEL // LANES, POOL_HALO, LANES), F32)
        kt_prev[...] = jnp.zeros_like(kt_prev)
        v_prev[:, :, 0:LANES] = jnp.zeros((4, BLOCK, LANES), BF16)
        lane = lax.broadcasted_iota(jnp.int32, (BLOCK, LANES), 1)
        for par in range(2):
            ones = jnp.where((lane >= HEAD_DIM) == (par == 1), 1.0, 0.0).astype(BF16)
            for kvh in range(N_KV_HEADS):
                v_prev[2 * kvh + par, :, LANES:] = ones
                for j in range(ts // BLOCK):
                    v_cur[2 * kvh + par, j * BLOCK:(j + 1) * BLOCK, LANES:] = ones

    x = x_ref[0]
    h = _rmsnorm_rows(x, g1_ref[...]).astype(BF16)
    zq = _dot(h, win_ref[:, O_Q:O_K])
    zkv = _dot(h, win_ref[:, O_K:O_G])

    ang = invf_ref[...] * pos_ref[0].astype(F32)
    cs = jnp.concatenate([jnp.cos(ang), jnp.sin(ang)], axis=0)
    cs_hi = cs.astype(BF16)
    cs_lo = (cs - cs_hi.astype(F32)).astype(BF16)
    tables = _dot_tn(jnp.concatenate([cs_hi, cs_lo], axis=0), expand_ref[...])
    cos_t = tables[:, 0:LANES] + passlane_ref[...]
    sin_fwd = tables[:, LANES:2 * LANES]
    sin_bwd = tables[:, 2 * LANES:3 * LANES]
    q_tabs = (cos_t * gain_ref[0:1, :], sin_fwd * gain_ref[1:2, :], sin_bwd * gain_ref[2:3, :])
    k_tabs = (cos_t * gain_ref[3:4, :], sin_fwd * gain_ref[4:5, :], sin_bwd * gain_ref[5:6, :])

    for c in range(Q_WIDTH // MXU_DIM):
        qs = zq[:, c * MXU_DIM:(c + 1) * MXU_DIM]
        r = lax.rsqrt(_dot((qs * qs).astype(BF16), bd_ref[...]) + EPS)
        for hf in range(MXU_DIM // LANES):
            sl = slice(hf * LANES, (hf + 1) * LANES)
            lo = c * MXU_DIM + hf * LANES
            q_scr[:, lo:lo + LANES] = (_rope(qs[:, sl], *q_tabs) * r[:, sl]).astype(BF16)

    ks = zkv[:, 0:KV_WIDTH]
    rk = lax.rsqrt(_dot((ks * ks).astype(BF16), bd_ref[0:LANES, 0:LANES]) + EPS)
    kt = (_rope(ks, *k_tabs) * rk).T.astype(BF16)
    zero_half = jnp.zeros((HEAD_DIM, ts), BF16)
    for kvh in range(N_KV_HEADS):
        half = kt[kvh * HEAD_DIM:(kvh + 1) * HEAD_DIM]
        kt_cur[2 * kvh] = jnp.concatenate([half, zero_half], axis=0)
        kt_cur[2 * kvh + 1] = jnp.concatenate([zero_half, half], axis=0)
    vs = zkv[:, KV_WIDTH:]
    low = lax.broadcasted_iota(jnp.int32, (ts, LANES), 1) < HEAD_DIM
    v_swapped = pltpu.roll(vs, HEAD_DIM, 1)
    v_cur[0, :, 0:LANES] = jnp.where(low, vs, 0.0).astype(BF16)
    v_cur[1, :, 0:LANES] = jnp.where(low, 0.0, v_swapped).astype(BF16)
    v_cur[2, :, 0:LANES] = jnp.where(low, v_swapped, 0.0).astype(BF16)
    v_cur[3, :, 0:LANES] = jnp.where(low, 0.0, vs).astype(BF16)

    zu = _dot(h, win_ref[:, 0:O_Q])
    row_head = lax.broadcasted_iota(jnp.int32, (POOL_HEAD, LANES), 0) + t * ts
    a_parts = []
    for g, w in enumerate(POOL_WINDOWS):
        n_stage = g + 1
        pooled = []
        for s in range(POOL_GROUP // LANES):
            slab = g * (POOL_GROUP // LANES) + s
            u = zu[:, slab * LANES:(slab + 1) * LANES]
            pool_scr[0, slab, POOL_HALO:, :] = u
            for k in range(1, n_stage):
                sh = 2 ** (k - 1)
                lo = SUBLANES * k
                pool_scr[k, slab, lo:, :] = (pool_scr[k - 1, slab, lo:, :]
                                             + pool_scr[k - 1, slab, lo - sh:POOL_HALO + ts - sh, :])
            sh = w // 2
            wsum = (pool_scr[n_stage - 1, slab, POOL_HALO:, :]
                    + pool_scr[n_stage - 1, slab, POOL_HALO - sh:POOL_HALO + ts - sh, :])
            cnt = jnp.minimum(row_head + 1, w).astype(F32)
            head = wsum[:POOL_HEAD] / cnt
            tail = wsum[POOL_HEAD:] * (1.0 / w)
            pooled.append(jnp.concatenate([head, tail], axis=0) - u)
            pool_scr[0, slab, 0:POOL_HALO, :] = u[ts - POOL_HALO:]
        cols = slice(g * POOL_GROUP, (g + 1) * POOL_GROUP)
        mixed = _dot(jnp.concatenate(pooled, axis=1).astype(BF16), wpool_ref[g])
        a_parts.append(mixed * ps_ref[:, cols])

    gates = jax.nn.sigmoid(_dot(h, win_ref[:, O_G:]) + bg_ref[...])

    lane_low = lax.broadcasted_iota(jnp.int32, (BLOCK, LANES), 1) < HEAD_DIM
    for j in range(ts // BLOCK):
        if j == 0:
            bias = jnp.where(t == 0, bias_ref[1], bias_ref[0]).astype(BF16)
        else:
            bias = bias_ref[0].astype(BF16)
        rows = slice(j * BLOCK, (j + 1) * BLOCK)
        band = slice((j - 1) * BLOCK, (j + 1) * BLOCK)
        for kvh in range(N_KV_HEADS):
            base = kvh * GQA_GROUP * HEAD_DIM
            qstack = jnp.concatenate(
                [q_scr[rows, base + p * LANES:base + (p + 1) * LANES] for p in range(pairs)], axis=0)
            lhs = jnp.concatenate([qstack, ident_ref[...]], axis=1)
            acc = None
            maxes = []
            for par in range(2):
                var = 2 * kvh + par
                if j == 0:
                    k_band = jnp.concatenate([kt_prev[var], kt_cur[var, :, 0:BLOCK]], axis=1)
                    v_band = jnp.concatenate([v_prev[var], v_cur[var, 0:BLOCK, :]], axis=0)
                else:
                    k_band = kt_cur[var, :, band]
                    v_band = v_cur[var, band, :]
                s = _dot(lhs, jnp.concatenate([k_band, bias], axis=0))
                probs = []
                for p in range(pairs):
                    sp = s[p * BLOCK:(p + 1) * BLOCK]
                    m = jnp.max(sp, axis=-1, keepdims=True)
                    maxes.append(m)
                    probs.append(jnp.exp2(sp - m).astype(BF16))
                pv = _dot(jnp.concatenate(probs, axis=0), v_band)
                acc = pv if acc is None else acc + pv
            for p in range(pairs):
                blk = acc[p * BLOCK:(p + 1) * BLOCK]
                head = kvh * GQA_GROUP + 2 * p
                sink = jnp.where(lane_low[0:1], sinks_ref[head] * LOG2E, sinks_ref[head + 1] * LOG2E)
                denom = blk[:, LANES:] + jnp.exp2(sink - jnp.where(lane_low, maxes[p], maxes[pairs + p]))
                b_scr[rows, base + p * LANES:base + (p + 1) * LANES] = blk[:, :LANES] / denom

    a = jnp.concatenate(a_parts, axis=1)
    y = gates[:, :D_MODEL] * a + gates[:, D_MODEL:] * b_scr[...]
    o_ref[0] = x + _dot(y.astype(BF16), wout_ref[...])


def _ffn_kernel(x_ref, g2_ref, wup_ref, cw_ref, cb_ref, wdown_ref, o_ref, carry, ring, act_scr):
    ts = x_ref.shape[1]
    t = pl.program_id(1)

    @pl.when(t == 0)
    def _():
        carry[...] = jnp.zeros_like(carry)

    x = x_ref[0]
    h = _rmsnorm_rows(x, g2_ref[...]).astype(BF16)

    per_call = FF_CHUNK // LANES

    def conv(col0, call):
        up = _dot(h, wup_ref[:, col0:col0 + FF_CHUNK])
        parts = []
        for s in range(per_call):
            slab = (col0 + s * LANES) // LANES
            r = (call * per_call + s) % RING_SLABS
            cols = slice(col0 + s * LANES, col0 + (s + 1) * LANES)
            cur = up[:, s * LANES:(s + 1) * LANES]
            ring[r, 0:CONV_HALO, :] = carry[slab]
            ring[r, CONV_HALO:, :] = cur
            parts.append(cb_ref[:, cols] + cw_ref[2:3, cols] * cur
                         + cw_ref[1:2, cols] * ring[r, CONV_HALO - 1:CONV_HALO - 1 + ts, :]
                         + cw_ref[0:1, cols] * ring[r, CONV_HALO - 2:CONV_HALO - 2 + ts, :])
            carry[slab] = cur[ts - CONV_HALO:]
        return jnp.concatenate(parts, axis=1)

    for c in range(D_FF // FF_CHUNK):
        gate = conv(c * FF_CHUNK, 2 * c)
        val = conv(D_FF + c * FF_CHUNK, 2 * c + 1)
        act_scr[:, c * FF_CHUNK:(c + 1) * FF_CHUNK] = (gate * jax.nn.sigmoid(gate) * val).astype(BF16)

    o_ref[0] = x + _dot(act_scr[...], wdown_ref[...])


def _const_spec(shape):
    nd = len(shape)
    return pl.BlockSpec(shape, lambda b, t: (0,) * nd, pipeline_mode=pl.Buffered(1))


def _attention_bias():
    i = np.arange(BLOCK)[:, None]
    c = np.arange(2 * BLOCK)[None, :]
    ok = (c > i) & (c <= i + BLOCK)
    general = np.where(ok, 0.0, NEG_BIG)
    first = np.where(ok & (c >= BLOCK), 0.0, NEG_BIG)
    return jnp.asarray(np.stack([general, first]), F32)


def _rope_constants():
    inv_freq = ROPE_THETA ** (-np.arange(0, ROPE_DIM, 2, dtype=np.float32) / ROPE_DIM)
    d = np.arange(LANES) % HEAD_DIM
    expand = np.zeros((4 * ROPE_HALF, 3 * LANES), np.float32)
    for f in range(ROPE_HALF):
        for part in (0, 2 * ROPE_HALF):
            expand[part + f, np.nonzero((d == f) | (d == f + ROPE_HALF))[0]] = 1.0
            expand[part + ROPE_HALF + f, LANES + np.nonzero(d == f + ROPE_HALF)[0]] = 1.0
            expand[part + ROPE_HALF + f, 2 * LANES + np.nonzero(d == f)[0]] = -1.0
    passlane = (d >= ROPE_DIM).astype(np.float32)[None, :]
    return (jnp.asarray(inv_freq[:, None], F32), jnp.asarray(expand, BF16), jnp.asarray(passlane, F32))


def _rope_gains(q_norm, k_norm):
    def rows(g):
        return [jnp.tile(g, LANES // HEAD_DIM), jnp.tile(jnp.roll(g, ROPE_HALF), LANES // HEAD_DIM),
                jnp.tile(jnp.roll(g, -ROPE_HALF), LANES // HEAD_DIM)]
    return jnp.stack(rows(q_norm * (HEAD_DIM ** -0.5 * LOG2E)) + rows(k_norm)).astype(F32)


def kernel(x, positions, attn_norm, w_in, b_gate, w_pool, pool_scale, q_norm, k_norm, sinks, w_out, ffn_norm,
           w_up, conv_w, conv_b, w_down):
    B, S, D = x.shape
    assert D == D_MODEL and S % MIXER_TILE == 0 and MIXER_TILE % BLOCK == 0 and S % FFN_TILE == 0
    assert attn_norm.shape[0] == 1, "single-layer block"
    ts = MIXER_TILE
    tf = FFN_TILE
    params = pltpu.CompilerParams(dimension_semantics=("arbitrary", "arbitrary"), vmem_limit_bytes=VMEM_LIMIT)

    invf_col, expand, passlane = _rope_constants()
    seg = np.arange(MXU_DIM) // HEAD_DIM
    block_mean = jnp.asarray((seg[:, None] == seg[None, :]) / HEAD_DIM, BF16)
    ident = jnp.asarray(np.tile(np.eye(BLOCK, dtype=np.float32), (GQA_GROUP // 2, 1)), BF16)
    n_stage = len(POOL_WINDOWS)
    n_t = S // ts
    n_steps = B * n_t
    up_rows, down_rows = D // n_steps, 64
    assert D % n_steps == 0 and up_rows % 16 == 0 and D_FF % down_rows == 0 and D_FF // down_rows <= n_steps
    last_down = D_FF // down_rows - 1

    const_shapes = ((1, D), (D, IN_WIDTH), (1, 2 * D), (4, POOL_GROUP, POOL_GROUP), (1, D), (6, LANES),
                    (ROPE_HALF, 1), (4 * ROPE_HALF, 3 * LANES), (1, LANES), (MXU_DIM, MXU_DIM),
                    (2, BLOCK, 2 * BLOCK), (GQA_GROUP // 2 * BLOCK, BLOCK), (D, D))
    hbm_shapes = ((D, IN_WIDTH), (4, POOL_GROUP, POOL_GROUP), (D, D))
    x1, w_up_bf, w_down_bf = pl.pallas_call(
        _mixer_kernel,
        grid_spec=pltpu.PrefetchScalarGridSpec(
            num_scalar_prefetch=1,
            grid=(B, S // ts),
            in_specs=[
                pl.BlockSpec((1, ts, D), lambda b, t, s: (b, t, 0)),
                pl.BlockSpec((1, 1, ts), lambda b, t, s: (b, 0, t)),
            ] + [pl.BlockSpec(memory_space=pl.ANY) if shp in hbm_shapes else
                 pl.BlockSpec(shp, functools.partial(lambda nd, b, t, s: (0,) * nd, len(shp)),
                              pipeline_mode=pl.Buffered(1)) for shp in const_shapes] + [
                pl.BlockSpec((up_rows, 2 * D_FF), lambda b, t, s: (b * n_t + t, 0)),
                pl.BlockSpec((down_rows, D), lambda b, t, s: (jnp.minimum(b * n_t + t, last_down), 0)),
            ],
            out_specs=[
                pl.BlockSpec((1, ts, D), lambda b, t, s: (b, t, 0)),
                pl.BlockSpec((up_rows, 2 * D_FF), lambda b, t, s: (b * n_t + t, 0)),
                pl.BlockSpec((down_rows, D), lambda b, t, s: (jnp.minimum(b * n_t + t, last_down), 0)),
            ],
            scratch_shapes=[
                pltpu.VMEM((n_stage, D // LANES, POOL_HALO + ts, LANES), F32),
                pltpu.VMEM((4, LANES, BLOCK), BF16),
                pltpu.VMEM((4, LANES, ts), BF16),
                pltpu.VMEM((4, BLOCK, 2 * LANES), BF16),
                pltpu.VMEM((4, ts, 2 * LANES), BF16),
                pltpu.VMEM((ts, Q_WIDTH), BF16),
                pltpu.VMEM((ts, Q_WIDTH), F32),
                pltpu.VMEM((D, IN_WIDTH), BF16),
                pltpu.VMEM((4, POOL_GROUP, POOL_GROUP), BF16),
                pltpu.VMEM((D, D), BF16),
                pltpu.VMEM((STAGE_SLOTS, WEIGHT_ROWS, IN_WIDTH), F32),
                pltpu.VMEM((D, D), F32),
                pltpu.VMEM((4, POOL_GROUP, POOL_GROUP), F32),
                pltpu.SemaphoreType.DMA((STAGE_SLOTS + 2,)),
            ]),
        out_shape=[jax.ShapeDtypeStruct((B, S, D), F32), jax.ShapeDtypeStruct((D, 2 * D_FF), BF16),
                   jax.ShapeDtypeStruct((D_FF, D), BF16)],
        compiler_params=params,
        name="mixer",
    )(sinks[0], x, positions.reshape(B, 1, S), attn_norm, w_in[0], b_gate, w_pool[0],
      pool_scale, _rope_gains(q_norm[0], k_norm[0]), invf_col, expand, passlane, block_mean, _attention_bias(),
      ident, w_out[0], w_up[0], w_down[0])

    out = pl.pallas_call(
        _ffn_kernel,
        grid=(B, S // tf),
        in_specs=[pl.BlockSpec((1, tf, D), lambda b, t: (b, t, 0)), _const_spec((1, D)), _const_spec((D, 2 * D_FF)),
                  _const_spec((3, 2 * D_FF)), _const_spec((1, 2 * D_FF)), _const_spec((D_FF, D))],
        out_specs=pl.BlockSpec((1, tf, D), lambda b, t: (b, t, 0)),
        out_shape=jax.ShapeDtypeStruct((B, S, D), F32),
        scratch_shapes=[pltpu.VMEM((2 * D_FF // LANES, CONV_HALO, LANES), F32),
                        pltpu.VMEM((RING_SLABS, CONV_HALO + tf, LANES), F32), pltpu.VMEM((tf, D_FF), BF16)],
        compiler_params=params,
        name="ffn",
    )(x1, ffn_norm, w_up_bf, conv_w[0], conv_b, w_down_bf)
    return out
```
